```python
import jax, jax.numpy as jnp
from jax import lax
import numpy as np

D_MODEL = 1024
BATCH = 16
SEQ = 2048
DEPTH = 2

HEAD_DIM = 64
NSA_Q_HEADS = 8
NSA_KV_HEADS = 2
SWA_Q_HEADS = 8
SWA_KV_HEADS = 2
MIX_WIDTH = (NSA_Q_HEADS + SWA_Q_HEADS) * HEAD_DIM
ROPE_THETA = 10000.0
CMP_STRIDE = 16
CMP_BLOCK = 2 * CMP_STRIDE
CMP_HIDDEN = 256
SEL_BLOCK = 64
SEL_TOPN = 8
NSA_WINDOW = 512
SWA_WINDOW = 128
BAND_BLOCK = 128
SEL_Q_CHUNK = 128
N_EXPERTS = 32
TOP_K = 4
D_FF = D_MODEL
SWIGLU_LIMIT = 7.0
SWIGLU_ALPHA = 1.702
P_DIM = 256
DN_ALPHA = (2 * DEPTH) ** 0.25
DN_BETA = (8 * DEPTH) ** -0.25
LN_EPS = 1e-5
NEG = -1e30
FORCE = 1e9
KV_W = NSA_KV_HEADS * HEAD_DIM
IN_SIZES = [NSA_Q_HEADS * HEAD_DIM] + [KV_W] * 6 + [NSA_Q_HEADS * 3] + [SWA_Q_HEADS * HEAD_DIM] + [SWA_KV_HEADS * HEAD_DIM] * 2
IN_COLS = sum(IN_SIZES)
IN_SPLITS = [int(v) for v in np.cumsum(IN_SIZES)[:-1]]

kernel_name = "nsa_swa_sink_hymba_moe_deepnorm"


def layer_norm(x, g, b):
    xf = x.astype(jnp.float32)
    mu = xf.mean(-1, keepdims=True)
    var = jnp.square(xf - mu).mean(-1, keepdims=True)
    return ((xf - mu) * lax.rsqrt(var + LN_EPS) * g + b).astype(x.dtype)


def rope_tables(pos):
    inv = 1.0 / (ROPE_THETA ** (jnp.arange(0, HEAD_DIM, 2, dtype=jnp.float32) / HEAD_DIM))
    ang = pos.astype(jnp.float32)[..., None] * inv
    return jnp.cos(ang), jnp.sin(ang)


def apply_rope(x, cos, sin):
    shp = cos.shape[:2] + (1,) * (x.ndim - 3) + cos.shape[2:]
    c, s = cos.reshape(shp), sin.reshape(shp)
    x1, x2 = jnp.split(x.astype(jnp.float32), 2, axis=-1)
    return jnp.concatenate([x1 * c - x2 * s, x2 * c + x1 * s], -1).astype(x.dtype)


def masked_softmax(s, mask, sink=None):
    s = jnp.where(mask, s.astype(jnp.float32), NEG)
    m = s.max(-1, keepdims=True)
    if sink is not None:
        m = jnp.maximum(m, sink)
    e = jnp.where(mask, jnp.exp(s - m), 0.0)
    den = e.sum(-1, keepdims=True)
    if sink is not None:
        den = den + jnp.exp(sink - m)
    return e / jnp.maximum(den, 1e-30)


def banded_attention(q, k, v, window, sinks=None):
    B, T, G, R, dh = q.shape
    nb = T // BAND_BLOCK
    nprev = -(-(window - 1) // BAND_BLOCK)
    pad = nprev * BAND_BLOCK

    def windows(a):
        ap = jnp.pad(a, ((0, 0), (pad, 0), (0, 0), (0, 0)))
        ab = ap.reshape(B, nb + nprev, BAND_BLOCK, G, dh)
        return jnp.concatenate([ab[:, i:i + nb] for i in range(nprev + 1)], axis=2)

    kw, vw = windows(k), windows(v)
    qb = q.reshape(B, nb, BAND_BLOCK, G, R, dh)
    s = jnp.einsum('bnqgrd,bnkgd->bngrqk', qb, kw) * (dh ** -0.5)
    blk = jnp.arange(nb)[:, None] * BAND_BLOCK
    qpos = blk + jnp.arange(BAND_BLOCK)
    kpos = blk - pad + jnp.arange((nprev + 1) * BAND_BLOCK)
    diff = qpos[:, :, None] - kpos[:, None, :]
    mask = (kpos[:, None, :] >= 0) & (diff >= 0) & (diff < window)
    mask = mask[None, :, None, None]
    sink = None if sinks is None else sinks.astype(jnp.float32).reshape(G, R)[None, None, :, :, None, None]
    p = masked_softmax(s, mask, sink)
    o = jnp.einsum('bngrqk,bnkgd->bnqgrd', p.astype(v.dtype), vw)
    return o.reshape(B, T, G, R, dh)


def compress(a, pe, w1, w2):
    B, T, G, dh = a.shape
    ch = a.reshape(B, T // CMP_STRIDE, CMP_STRIDE, G, dh)
    blk = jnp.concatenate([ch[:, :-1], ch[:, 1:]], axis=2) + pe[None, None, :, None, :]
    nc = blk.shape[1]
    flat = jnp.moveaxis(blk, 3, 2).reshape(B, nc, G, CMP_BLOCK * dh)
    return jax.nn.gelu(flat @ w1) @ w2


def nsa_mixer(q, k_c, v_c, k_s, v_s, k_w, v_w, gates, positions, ck_pe, w_ck1, w_ck2, cv_pe, w_cv1, w_cv2):
    B, T, G, R, dh = q.shape
    scale = dh ** -0.5
    kc = compress(k_c, ck_pe, w_ck1, w_ck2)
    vc = compress(v_c, cv_pe, w_cv1, w_cv2)
    nc = kc.shape[1]
    cos_c, sin_c = rope_tables(positions[:, CMP_BLOCK - 1::CMP_STRIDE])
    kc = apply_rope(kc, cos_c, sin_c)
    t = jnp.arange(T)
    cend = jnp.arange(nc) * CMP_STRIDE + CMP_BLOCK - 1
    cmask = (cend[None, :] <= t[:, None])[None, None, None]
    p_cmp = masked_softmax(jnp.einsum('btgrd,bcgd->bgrtc', q, kc) * scale, cmask)
    o_cmp = jnp.einsum('bgrtc,bcgd->btgrd', p_cmp.astype(vc.dtype), vc)
    nsel = T // SEL_BLOCK
    imp = jnp.pad(p_cmp.sum(2), ((0, 0), (0, 0), (0, 0), (0, 1)))
    imp = imp.reshape(B, G, T, nsel, SEL_BLOCK // CMP_STRIDE).sum(-1)
    j = jnp.arange(nsel)[None, :]
    cur = t[:, None] // SEL_BLOCK
    forced = (j == 0) | (j == cur) | (j == cur - 1)
    valid = j * SEL_BLOCK <= t[:, None]
    score = jnp.where(forced, FORCE, jnp.where(valid, imp, -1.0))
    topn = min(SEL_TOPN, nsel)
    _, idx = lax.top_k(score, topn)
    kb = jnp.moveaxis(k_s.reshape(B, nsel, SEL_BLOCK, G, dh), 3, 1)
    vb = jnp.moveaxis(v_s.reshape(B, nsel, SEL_BLOCK, G, dh), 3, 1)
    C = SEL_Q_CHUNK
    nq = T // C
    qc = jnp.moveaxis(q.reshape(B, nq, C, G, R, dh), 1, 0)
    ic = jnp.moveaxis(idx.reshape(B, G, nq, C, topn), 2, 0)
    bi = jnp.arange(B)[:, None, None, None]
    gi = jnp.arange(G)[None, :, None, None]

    def sel_chunk(args):
        ci, qx, ix = args
        ks = kb[bi, gi, ix].reshape(B, G, C, topn * SEL_BLOCK, dh)
        vs = vb[bi, gi, ix].reshape(B, G, C, topn * SEL_BLOCK, dh)
        kpos = (ix[..., None] * SEL_BLOCK + jnp.arange(SEL_BLOCK)).reshape(B, G, C, topn * SEL_BLOCK)
        tpos = ci * C + jnp.arange(C)
        mask = kpos[:, :, None] <= tpos[None, None, None, :, None]
        ps = masked_softmax(jnp.einsum('bcgrd,bgckd->bgrck', qx, ks) * scale, mask)
        return jnp.einsum('bgrck,bgckd->bcgrd', ps.astype(vs.dtype), vs)

    o_slc = lax.map(sel_chunk, (jnp.arange(nq), qc, ic))
    o_slc = jnp.moveaxis(o_slc, 0, 1).reshape(B, T, G, R, dh)
    o_win = banded_attention(q, k_w, v_w, NSA_WINDOW)
    g = jax.nn.sigmoid(gates.astype(jnp.float32)).reshape(B, T, G, R, 3)
    o = g[..., 0:1] * o_cmp + g[..., 1:2] * o_slc + g[..., 2:3] * o_win
    return o.astype(q.dtype).reshape(B, T, NSA_Q_HEADS * dh)


def token_mixers(x, cos, sin, positions, w_in, b_in, ck_pe, w_ck1, w_ck2, cv_pe, w_cv1, w_cv2, sinks, w_o, b_o):
    B, T, _ = x.shape
    dh = HEAD_DIM
    h = x @ w_in + b_in
    (qn, kc, vc, ks, vs, kw, vw, gates, qs, k_swa, v_swa) = jnp.split(h, IN_SPLITS, axis=-1)
    Gn, Rn = NSA_KV_HEADS, NSA_Q_HEADS // NSA_KV_HEADS
    qn = apply_rope(qn.reshape(B, T, NSA_Q_HEADS, dh), cos, sin).reshape(B, T, Gn, Rn, dh)
    kv = lambda a: a.reshape(B, T, Gn, dh)
    o_nsa = nsa_mixer(qn, kv(kc), kv(vc), apply_rope(kv(ks), cos, sin), kv(vs),
                      apply_rope(kv(kw), cos, sin), kv(vw), gates, positions,
                      ck_pe, w_ck1, w_ck2, cv_pe, w_cv1, w_cv2)
    Gs, Rs = SWA_KV_HEADS, SWA_Q_HEADS // SWA_KV_HEADS
    qs = apply_rope(qs.reshape(B, T, SWA_Q_HEADS, dh), cos, sin).reshape(B, T, Gs, Rs, dh)
    k_swa = apply_rope(k_swa.reshape(B, T, Gs, dh), cos, sin)
    o_swa = banded_attention(qs, k_swa, v_swa.reshape(B, T, Gs, dh), SWA_WINDOW, sinks)
    o = jnp.concatenate([o_nsa, o_swa.reshape(B, T, SWA_Q_HEADS * dh)], axis=-1)
    return o @ w_o + b_o


def moe(x, w_r, b_r, w_e1, b_e1, w_e2, b_e2):
    B, T, D = x.shape
    xt = x.reshape(B * T, D)
    logits = (xt @ w_r + b_r).astype(jnp.float32)
    vals, idx = lax.top_k(logits, TOP_K)
    gw = jax.nn.softmax(vals, axis=-1)
    combine = (jax.nn.one_hot(idx, N_EXPERTS, dtype=jnp.float32) * gw[..., None]).sum(1)
    y = jnp.zeros((B * T, D), jnp.float32)
    for e in range(N_EXPERTS):
        hh = xt @ w_e1[e] + b_e1[e]
        gate = jnp.minimum(hh[:, :D_FF], SWIGLU_LIMIT)
        up = jnp.clip(hh[:, D_FF:], -SWIGLU_LIMIT, SWIGLU_LIMIT)
        act = (up + 1.0) * gate * jax.nn.sigmoid(SWIGLU_ALPHA * gate)
        y = y + combine[:, e:e + 1] * (act @ w_e2[e] + b_e2[e])
    return y.astype(x.dtype).reshape(B, T, D)


def setup_inputs(seed: int = 0) -> dict:
    key = jax.random.key(seed)
    ks = jax.random.split(key, 27)
    n = lambda k, shp, sc: jax.random.normal(k, shp, jnp.float32) * sc
    L, D, dh = DEPTH, D_MODEL, HEAD_DIM
    return {
        "x": n(ks[0], (BATCH, SEQ, D), 1.0),
        "p": n(ks[1], (L, BATCH, SEQ, P_DIM), 1.0),
        "positions": (jnp.arange(SEQ, dtype=jnp.int32)[None, :] + jax.random.randint(ks[2], (BATCH, 1), 0, 4096, dtype=jnp.int32)),
        "w_in": n(ks[3], (L, D, IN_COLS), D ** -0.5),
        "b_in": n(ks[4], (L, IN_COLS), 0.02),
        "ck_pe": n(ks[5], (L, CMP_BLOCK, dh), 0.02),
        "w_ck1": n(ks[6], (L, CMP_BLOCK * dh, CMP_HIDDEN), (CMP_BLOCK * dh) ** -0.5),
        "w_ck2": n(ks[7], (L, CMP_HIDDEN, dh), CMP_HIDDEN ** -0.5),
        "cv_pe": n(ks[8], (L, CMP_BLOCK, dh), 0.02),
        "w_cv1": n(ks[9], (L, CMP_BLOCK * dh, CMP_HIDDEN), (CMP_BLOCK * dh) ** -0.5),
        "w_cv2": n(ks[10], (L, CMP_HIDDEN, dh), CMP_HIDDEN ** -0.5),
        "sinks": n(ks[11], (L, SWA_Q_HEADS), 0.5),
        "w_o": n(ks[12], (L, MIX_WIDTH, D), MIX_WIDTH ** -0.5 * DN_BETA),
        "b_o": n(ks[13], (L, D), 0.01),
        "ln1_g": 1.0 + n(ks[14], (L, D), 0.01),
        "ln1_b": n(ks[15], (L, D), 0.01),
        "w_r": n(ks[16], (L, D, N_EXPERTS), D ** -0.5),
        "b_r": n(ks[17], (L, N_EXPERTS), 0.01),
        "w_e1": n(ks[18], (L, N_EXPERTS, D, 2 * D_FF), D ** -0.5),
        "b_e1": n(ks[19], (L, N_EXPERTS, 2 * D_FF), 0.01),
        "w_e2": n(ks[20], (L, N_EXPERTS, D_FF, D), D_FF ** -0.5 * DN_BETA),
        "b_e2": n(ks[21], (L, N_EXPERTS, D), 0.01),
        "w_pg": n(ks[22], (L, D, D), D ** -0.5),
        "b_pg": n(ks[23], (L, D), 0.01),
        "w_pp": n(ks[24], (L, P_DIM, D), P_DIM ** -0.5 * DN_BETA),
        "ln2_g": 1.0 + n(ks[25], (L, D), 0.01),
        "ln2_b": n(ks[26], (L, D), 0.01),
    }


def reference(x, p, positions, w_in, b_in, ck_pe, w_ck1, w_ck2, cv_pe, w_cv1, w_cv2, sinks, w_o, b_o,
              ln1_g, ln1_b, w_r, b_r, w_e1, b_e1, w_e2, b_e2, w_pg, b_pg, w_pp, ln2_g, ln2_b):
    cos, sin = rope_tables(positions)
    for i in range(DEPTH):
        a = token_mixers(x, cos, sin, positions, w_in[i], b_in[i], ck_pe[i], w_ck1[i], w_ck2[i],
                         cv_pe[i], w_cv1[i], w_cv2[i], sinks[i], w_o[i], b_o[i])
        x = layer_norm(DN_ALPHA * x + a, ln1_g[i], ln1_b[i])
        m = moe(x, w_r[i], b_r[i], w_e1[i], b_e1[i], w_e2[i], b_e2[i])
        e = jax.nn.sigmoid(x @ w_pg[i] + b_pg[i]) * (p[i] @ w_pp[i])
        x = layer_norm(DN_ALPHA * x + m + e, ln2_g[i], ln2_b[i])
    return x
```

```python
import functools

import numpy as np
import jax
import jax.numpy as jnp
from jax import lax
from jax.experimental import pallas as pl
from jax.experimental.pallas import tpu as pltpu

F32 = jnp.float32
BF16 = jnp.bfloat16

D_MODEL = 1024
DEPTH = 2
HEAD_DIM = 64
Q_HEADS = 8
KV_GROUPS = 2
GROUP_HEADS = Q_HEADS // KV_GROUPS
ROPE_THETA = 10000.0
CMP_STRIDE = 16
CMP_BLOCK = 32
CMP_HIDDEN = 256
SEL_BLOCK = 64
SEL_TOPN = 8
NSA_WINDOW = 512
SWA_WINDOW = 128
N_EXPERTS = 32
TOP_K = 4
D_FF = D_MODEL
SWIGLU_LIMIT = 7.0
SWIGLU_ALPHA = 1.702
P_DIM = 256
DN_ALPHA = (2 * DEPTH) ** 0.25
LN_EPS = 1e-5
NEG = -1e30
FORCE = 1e9

LANES = 128
Q_CHUNK = 128
KV_W = KV_GROUPS * HEAD_DIM
Q_W = Q_HEADS * HEAD_DIM
VMEM_LIMIT = 56 * 1024 * 1024

_OFF_QN, _OFF_KC, _OFF_VC, _OFF_KS, _OFF_VS, _OFF_KW, _OFF_VW = 0, 512, 640, 768, 896, 1024, 1152
_OFF_GATES, _OFF_QS, _OFF_KSWA, _OFF_VSWA = 1280, 1304, 1816, 1944
N_GATES = Q_HEADS * 3

N_ROPE_TILES = 11
N_PLAIN_TILES = 6
ROPE_W = N_ROPE_TILES * LANES
PROJ_W = 2 * ROPE_W + N_PLAIN_TILES * LANES


def _paired_head_cols(base):
    cols = []
    for j in range(GROUP_HEADS):
        for half in range(KV_GROUPS):
            h = j + GROUP_HEADS * half
            cols.extend(base + h * HEAD_DIM + d for d in range(HEAD_DIM))
    return cols


def _proj_layout():
    rope_cols = (_paired_head_cols(_OFF_QN) + _paired_head_cols(_OFF_QS)
                 + list(range(_OFF_KS, _OFF_KS + KV_W)) + list(range(_OFF_KW, _OFF_KW + KV_W))
                 + list(range(_OFF_KSWA, _OFF_KSWA + KV_W)))
    rope_cols = np.asarray(rope_cols, np.int32)
    scale = np.ones(ROPE_W, np.float32)
    scale[:2 * Q_W] = HEAD_DIM ** -0.5
    pos_in_tile = np.arange(ROPE_W) % HEAD_DIM
    first_half = pos_in_tile < HEAD_DIM // 2
    rot_cols = np.where(first_half, rope_cols + HEAD_DIM // 2, rope_cols - HEAD_DIM // 2).astype(np.int32)
    rot_sign = np.where(first_half, -1.0, 1.0).astype(np.float32)
    plain_cols = np.concatenate([np.arange(o, o + KV_W) for o in (_OFF_KC, _OFF_VC, _OFF_VS, _OFF_VW, _OFF_VSWA)]
                                + [np.arange(_OFF_GATES, _OFF_GATES + N_GATES)]).astype(np.int32)
    return rope_cols, scale, rot_cols, rot_sign * scale, plain_cols


_ROPE_COLS, _ROPE_SCALE, _ROT_COLS, _ROT_SCALE, _PLAIN_COLS = _proj_layout()
_WO_ROWS = np.asarray(_paired_head_cols(0) + _paired_head_cols(Q_W), np.int32)


def _gate_expand_matrix():
    m = np.zeros((LANES, 3 * Q_W), np.float32)
    for br in range(3):
        for j in range(GROUP_HEADS):
            for lane in range(LANES):
                h = j + GROUP_HEADS * (lane // HEAD_DIM)
                m[h * 3 + br, br * Q_W + j * LANES + lane] = 1.0
    return m


def _pool_matrix():
    m = np.zeros((32, LANES), np.float32)
    for c in range(LANES - 1):
        m[c // (SEL_BLOCK // CMP_STRIDE), c] = 1.0
    return m


def _cparams(sem, vmem=VMEM_LIMIT):
    return pltpu.CompilerParams(dimension_semantics=sem, vmem_limit_bytes=vmem)


def _full(shape):
    return pl.BlockSpec(shape, lambda *_: (0,) * len(shape))


def _rope_table_kernel(pos_ref, inv_ref, cos_ref, sin_ref):
    ang = pos_ref[...] * inv_ref[...]
    cos_ref[...] = jnp.cos(ang)
    sin_ref[...] = jnp.sin(ang)


def rope_tables(pos, inv_tiled):
    m = pos.shape[0]
    pos_b = jnp.broadcast_to(pos.astype(F32)[:, None], (m, LANES))
    tm = min(m, 1024)
    spec = pl.BlockSpec((tm, LANES), lambda i: (i, 0))
    return pl.pallas_call(
        _rope_table_kernel,
        out_shape=(jax.ShapeDtypeStruct((m, LANES), F32),) * 2,
        grid=(m // tm,),
        in_specs=[spec, _full((1, LANES))],
        out_specs=(spec, spec),
        compiler_params=_cparams(("parallel",)),
        name="rope_tables",
    )(pos_b, inv_tiled)


def _inproj_kernel(x_ref, w_ref, b_ref, cos_ref, sin_ref,
                   qn_ref, qs_ref, ks_ref, kw_ref, kswa_ref,
                   kc_ref, vc_ref, vs_ref, vw_ref, vswa_ref, gates_ref):
    xb = x_ref[...].astype(BF16)
    cos = cos_ref[...]
    sin = sin_ref[...]

    def proj(lo, hi):
        return jnp.dot(xb, w_ref[:, lo:hi], preferred_element_type=F32) + b_ref[:, lo:hi]

    def roped(tile_lo, n_tiles, out_ref):
        lo, hi = tile_lo * LANES, (tile_lo + n_tiles) * LANES
        h = proj(lo, hi)
        hr = proj(ROPE_W + lo, ROPE_W + hi)
        for t in range(n_tiles):
            sl = slice(t * LANES, (t + 1) * LANES)
            out_ref[:, sl] = (h[:, sl] * cos + hr[:, sl] * sin).astype(out_ref.dtype)

    roped(0, 4, qn_ref)
    roped(4, 4, qs_ref)
    roped(8, 1, ks_ref)
    roped(9, 1, kw_ref)
    roped(10, 1, kswa_ref)
    plain = proj(2 * ROPE_W, PROJ_W)
    for t, ref in enumerate((kc_ref, vc_ref, vs_ref, vw_ref, vswa_ref, gates_ref)):
        ref[...] = plain[:, t * LANES:(t + 1) * LANES].astype(ref.dtype)


def input_projection(x2d, w_all, b_all, cos, sin):
    n = x2d.shape[0]
    tm = min(n, 512)
    row = lambda w: pl.BlockSpec((tm, w), lambda i: (i, 0))
    outs = ([jax.ShapeDtypeStruct((n, Q_W), BF16)] * 2 + [jax.ShapeDtypeStruct((n, KV_W), BF16)] * 8
            + [jax.ShapeDtypeStruct((n, LANES), F32)])
    return pl.pallas_call(
        _inproj_kernel,
        out_shape=tuple(outs),
        grid=(n // tm,),
        in_specs=[row(D_MODEL), _full((D_MODEL, PROJ_W)), _full((1, PROJ_W)), row(LANES), row(LANES)],
        out_specs=tuple([row(Q_W)] * 2 + [row(KV_W)] * 8 + [row(LANES)]),
        compiler_params=_cparams(("parallel",)),
        name="input_projection",
    )(x2d, w_all, b_all, cos, sin)


def _prep_inproj_weights(w_in, b_in):
    w = jnp.concatenate([w_in[:, _ROPE_COLS] * _ROPE_SCALE, w_in[:, _ROT_COLS] * _ROT_SCALE,
                         w_in[:, _PLAIN_COLS], jnp.zeros((D_MODEL, LANES - N_GATES), F32)], axis=1)
    b = jnp.concatenate([b_in[_ROPE_COLS] * _ROPE_SCALE, b_in[_ROT_COLS] * _ROT_SCALE,
                         b_in[_PLAIN_COLS], jnp.zeros((LANES - N_GATES,), F32)])
    return w.astype(BF16), b[None, :]


def _gelu_tanh(x):
    return 0.5 * x * (1.0 + jnp.tanh(np.sqrt(2.0 / np.pi) * (x + 0.044715 * (x * x * x))))


def _compress_kernel(kc_ref, vc_ref, cos_ref, sin_ref,
                     kw1a_ref, kw1b_ref, kpe_ref, kw2_ref, kw2r_ref,
                     vw1a_ref, vw1b_ref, vpe_ref, vw2_ref,
                     ko_ref, vo_ref):
    def hidden(x_ref, w1a_ref, w1b_ref, pe_ref):
        x = x_ref[...]
        ya = jnp.dot(x, w1a_ref[...], preferred_element_type=F32)
        yb = jnp.dot(x, w1b_ref[...], preferred_element_type=F32)
        rows = ya.shape[0]
        h = ya + pltpu.roll(yb, rows - 1, 0)
        pe = pe_ref[...]
        peb = (jnp.dot(pe[:, :CMP_STRIDE * KV_W], w1a_ref[...], preferred_element_type=F32)
               + jnp.dot(pe[:, CMP_STRIDE * KV_W:], w1b_ref[...], preferred_element_type=F32))
        return _gelu_tanh(h + peb[0:1, :]).astype(BF16)

    ak = hidden(kc_ref, kw1a_ref, kw1b_ref, kpe_ref)
    kc = jnp.dot(ak, kw2_ref[...], preferred_element_type=F32)
    kcr = jnp.dot(ak, kw2r_ref[...], preferred_element_type=F32)
    ko_ref[...] = (kc * cos_ref[...] + kcr * sin_ref[...]).astype(ko_ref.dtype)
    av = hidden(vc_ref, vw1a_ref, vw1b_ref, vpe_ref)
    vo_ref[...] = jnp.dot(av, vw2_ref[...], preferred_element_type=F32).astype(vo_ref.dtype)


def _prep_compress_weights(pe, w1, w2, with_rot):
    eye = jnp.eye(KV_GROUPS, dtype=F32)
    w1r = w1.reshape(CMP_BLOCK, HEAD_DIM, CMP_HIDDEN)

    def half(wh):
        return jnp.einsum("idh,ge->igdeh", wh, eye).reshape(CMP_STRIDE * KV_W, KV_GROUPS * CMP_HIDDEN).astype(BF16)

    w1a, w1b = half(w1r[:CMP_STRIDE]), half(w1r[CMP_STRIDE:])
    pe_row = jnp.broadcast_to(pe[:, None, :], (CMP_BLOCK, KV_GROUPS, HEAD_DIM)).reshape(1, CMP_BLOCK * KV_W)
    pe_rows = jnp.broadcast_to(pe_row, (8, CMP_BLOCK * KV_W)).astype(BF16)
    w2b = jnp.einsum("hd,ge->gehd", w2, eye).reshape(KV_GROUPS, KV_GROUPS, CMP_HIDDEN, HEAD_DIM)
    w2b = jnp.transpose(w2b, (0, 2, 1, 3)).reshape(KV_GROUPS * CMP_HIDDEN, KV_W)
    out = [w1a, w1b, pe_rows, w2b.astype(BF16)]
    if with_rot:
        half_d = HEAD_DIM // 2
        w2rot = jnp.concatenate([-w2[:, half_d:], w2[:, :half_d]], axis=1)
        w2rb = jnp.einsum("hd,ge->gehd", w2rot, eye)
        w2rb = jnp.transpose(w2rb, (0, 2, 1, 3)).reshape(KV_GROUPS * CMP_HIDDEN, KV_W)
        out.append(w2rb.astype(BF16))
    return out


def compress_kv(kc_chunks, vc_chunks, cos_c, sin_c, kweights, vweights):
    rows = kc_chunks.shape[0]
    tm = min(rows, 512)
    cw = CMP_STRIDE * KV_W
    hw = KV_GROUPS * CMP_HIDDEN
    row = lambda w: pl.BlockSpec((tm, w), lambda i: (i, 0))
    wspecs_k = [_full((cw, hw)), _full((cw, hw)), _full((8, 2 * cw)), _full((hw, KV_W)), _full((hw, KV_W))]
    wspecs_v = wspecs_k[:4]
    return pl.pallas_call(
        _compress_kernel,
        out_shape=(jax.ShapeDtypeStruct((rows, KV_W), BF16),) * 2,
        grid=(rows // tm,),
        in_specs=[row(cw), row(cw), row(LANES), row(LANES)] + wspecs_k + wspecs_v,
        out_specs=(row(KV_W), row(KV_W)),
        compiler_params=_cparams(("parallel",)),
        name="compress_kv",
    )(kc_chunks, vc_chunks, cos_c, sin_c, *kweights, *vweights)


def _attn_kernel(sinks_ref, qn_ref, qs_ref, gates_ref, kc_ref, vc_ref,
                 ks_ref, vs_ref, kw_ref, vw_ref, ksw_ref, vsw_ref, pool_ref, gx_ref, o_ref):
    i = pl.program_id(1)
    c = Q_CHUNK
    nt_dims = (((1,), (1,)), ((), ()))
    row = lax.broadcasted_iota(jnp.int32, (c, c), 0)
    lane = lax.broadcasted_iota(jnp.int32, (c, c), 1)
    lo_half = lane < HEAD_DIM
    t_minus_key = row - lane
    group_mask = [jnp.where(lo_half, 1.0, 0.0).astype(BF16), jnp.where(lo_half, 0.0, 1.0).astype(BF16)]

    def rep(a):
        return jnp.concatenate([a] * GROUP_HEADS, axis=0)

    def stack_q(q_ref, g):
        return jnp.concatenate([q_ref[:, j * LANES:(j + 1) * LANES] * group_mask[g] for j in range(GROUP_HEADS)], axis=0)

    def flash_block(q, k_blk, v_blk, mask, carry):
        m, l, acc = carry
        s = lax.dot_general(q, k_blk, nt_dims, preferred_element_type=F32)
        mk = rep(mask)
        s = jnp.where(mk, s, NEG)
        m_new = jnp.maximum(m, jnp.max(s, axis=-1, keepdims=True))
        alpha = jnp.exp(m - m_new)
        p = jnp.where(mk, jnp.exp(s - m_new), 0.0)
        l = alpha * l + jnp.sum(p, axis=-1, keepdims=True)
        acc = alpha * acc + jnp.dot(p.astype(BF16), v_blk, preferred_element_type=F32)
        return m_new, l, acc

    def finish(carry):
        _, l, acc = carry
        return acc / jnp.maximum(l, 1e-30)

    rows4 = GROUP_HEADS * c
    init = (jnp.full((rows4, 1), NEG, F32), jnp.zeros((rows4, 1), F32), jnp.zeros((rows4, LANES), F32))

    def banded(q, k_ref, v_ref, window, carry):
        nprev = (window - 1 + c - 1) // c
        for back in range(nprev, -1, -1):
            kb = i - back
            ok = kb >= 0
            off = pl.multiple_of(jnp.maximum(kb, 0) * c, c)
            width = jnp.where(ok, window, 0).astype(jnp.uint32)
            mask = (t_minus_key + back * c).astype(jnp.uint32) < width
            carry = flash_block(q, k_ref[pl.ds(off, c), :], v_ref[pl.ds(off, c), :], mask, carry)
        return finish(carry)

    def compressed(q):
        s = lax.dot_general(q, kc_ref[...], nt_dims, preferred_element_type=F32)
        cend = lane * CMP_STRIDE + (CMP_BLOCK - 1)
        mk = rep(cend <= row + i * c)
        s = jnp.where(mk, s, NEG)
        m = jnp.max(s, axis=-1, keepdims=True)
        e = jnp.where(mk, jnp.exp(s - m), 0.0)
        p = e / jnp.maximum(jnp.sum(e, axis=-1, keepdims=True), 1e-30)
        o = jnp.dot(p.astype(BF16), vc_ref[...], preferred_element_type=F32)
        psum = p[0:c] + p[c:2 * c] + p[2 * c:3 * c] + p[3 * c:4 * c]
        return o, psum

    nsel = pool_ref.shape[0]
    blk = lax.broadcasted_iota(jnp.int32, (nsel, c), 0)
    tcol = lax.broadcasted_iota(jnp.int32, (nsel, c), 1)
    blk_of_lane = tcol >> 6

    def select_blocks(psum):
        hi = psum.astype(BF16)
        lo = (psum - hi.astype(F32)).astype(BF16)
        pool = pool_ref[...]
        imp = (lax.dot_general(pool, hi, nt_dims, preferred_element_type=F32)
               + lax.dot_general(pool, lo, nt_dims, preferred_element_type=F32))
        t = tcol + i * c
        cur = t >> 6
        forced = (blk == 0) | (blk == cur) | (blk == cur - 1)
        valid = (blk << 6) <= t
        score = jnp.where(forced, FORCE, jnp.where(valid, imp, -1.0))
        rank = jnp.zeros((nsel, c), F32)
        for k in range(nsel):
            sk = score[k:k + 1, :]
            tie = jnp.where(blk > k, 1.0, 0.0)
            rank = rank + jnp.where(sk > score, 1.0, jnp.where(sk == score, tie, 0.0))
        sel_t = jnp.where(rank < SEL_TOPN, 1.0, 0.0)
        return sel_t.T.astype(BF16)

    def selected(q, sel):
        def body(kb, carry):
            off = pl.multiple_of(kb * c, c)
            expand = jnp.where(blk == 2 * kb + blk_of_lane, 1.0, 0.0).astype(BF16)
            chosen = jnp.dot(sel, expand, preferred_element_type=F32) > 0.5
            mask = chosen & (t_minus_key + (i - kb) * c >= 0)
            return flash_block(q, ks_ref[pl.ds(off, c), :], vs_ref[pl.ds(off, c), :], mask, carry)
        return finish(lax.fori_loop(0, i + 1, body, init))

    gs = jax.nn.sigmoid(gates_ref[...])
    ghi = gs.astype(BF16)
    glo = (gs - ghi.astype(F32)).astype(BF16)
    gexp = (jnp.dot(ghi, gx_ref[...], preferred_element_type=F32)
            + jnp.dot(glo, gx_ref[...], preferred_element_type=F32))

    nsa = []
    for g in range(KV_GROUPS):
        q = stack_q(qn_ref, g)
        o_cmp, psum = compressed(q)
        o_slc = selected(q, select_blocks(psum))
        o_win = banded(q, kw_ref, vw_ref, NSA_WINDOW, init)
        nsa.append((o_cmp, o_slc, o_win))
    for j in range(GROUP_HEADS):
        rs = slice(j * c, (j + 1) * c)
        mix = jnp.zeros((c, LANES), F32)
        for br in range(3):
            o_pair = jnp.where(lo_half, nsa[0][br][rs], nsa[1][br][rs])
            mix = mix + gexp[:, br * Q_W + j * LANES: br * Q_W + (j + 1) * LANES] * o_pair
        o_ref[:, j * LANES:(j + 1) * LANES] = mix.astype(o_ref.dtype)

    head_row = lax.broadcasted_iota(jnp.int32, (rows4, 1), 0) // c
    swa = []
    for g in range(KV_GROUPS):
        sink = jnp.zeros((rows4, 1), F32)
        for r in range(GROUP_HEADS):
            sink = jnp.where(head_row == r, sinks_ref[g * GROUP_HEADS + r], sink)
        carry = (sink, jnp.ones((rows4, 1), F32), jnp.zeros((rows4, LANES), F32))
        swa.append(banded(stack_q(qs_ref, g), ksw_ref, vsw_ref, SWA_WINDOW, carry))
    for j in range(GROUP_HEADS):
        rs = slice(j * c, (j + 1) * c)
        o_ref[:, Q_W + j * LANES: Q_W + (j + 1) * LANES] = jnp.where(lo_half, swa[0][rs], swa[1][rs]).astype(o_ref.dtype)


def attention(sinks, qn, qs, gates, kc_c, vc_c, ks, vs, kw, vw, ksw, vsw, pool, gx, batch, seq):
    n = batch * seq
    nq = seq // Q_CHUNK
    assert seq // SEL_BLOCK == pool.shape[0] and seq // CMP_STRIDE == LANES
    qspec = lambda w: pl.BlockSpec((Q_CHUNK, w), lambda b, i, *_: (b * nq + i, 0))
    seqspec = pl.BlockSpec((seq, KV_W), lambda b, i, *_: (b, 0))
    cspec = pl.BlockSpec((LANES, KV_W), lambda b, i, *_: (b, 0))
    grid_spec = pltpu.PrefetchScalarGridSpec(
        num_scalar_prefetch=1,
        grid=(batch, nq),
        in_specs=[qspec(Q_W), qspec(Q_W), qspec(LANES), cspec, cspec] + [seqspec] * 6
                 + [_full(pool.shape), _full(gx.shape)],
        out_specs=qspec(2 * Q_W),
    )
    return pl.pallas_call(
        _attn_kernel,
        out_shape=jax.ShapeDtypeStruct((n, 2 * Q_W), BF16),
        grid_spec=grid_spec,
        compiler_params=_cparams(("parallel", "arbitrary")),
        name="attention",
    )(sinks, qn, qs, gates, kc_c, vc_c, ks, vs, kw, vw, ksw, vsw, pool, gx)


def _layer_norm(y, g, b):
    mu = jnp.mean(y, axis=-1, keepdims=True)
    yc = y - mu
    var = jnp.mean(yc * yc, axis=-1, keepdims=True)
    return yc * lax.rsqrt(var + LN_EPS) * g + b


def _post_attn_kernel(o_ref, x_ref, p_ref, wo_ref, bo_ref, g1_ref, b1_ref,
                      wrh_ref, wrl_ref, br_ref, wpg_ref, bpg_ref, wpp_ref,
                      x1_ref, e_ref, route_ref):
    a = jnp.dot(o_ref[...], wo_ref[...], preferred_element_type=F32) + bo_ref[...]
    x1 = _layer_norm(DN_ALPHA * x_ref[...] + a, g1_ref[...], b1_ref[...])
    x1_ref[...] = x1
    xh = x1.astype(BF16)
    gate = jax.nn.sigmoid(jnp.dot(xh, wpg_ref[...], preferred_element_type=F32) + bpg_ref[...])
    e_ref[...] = gate * jnp.dot(p_ref[...].astype(BF16), wpp_ref[...], preferred_element_type=F32)
    xl = (x1 - xh.astype(F32)).astype(BF16)
    nt_dims = (((1,), (1,)), ((), ()))
    logits = (lax.dot_general(wrh_ref[...], xh, nt_dims, preferred_element_type=F32)
              + lax.dot_general(wrh_ref[...], xl, nt_dims, preferred_element_type=F32)
              + lax.dot_general(wrl_ref[...], xh, nt_dims, preferred_element_type=F32)) + br_ref[...]
    eidx = lax.broadcasted_iota(jnp.int32, logits.shape, 0).astype(F32)
    vals, idxs = [], []
    for _ in range(TOP_K):
        v = jnp.max(logits, axis=0, keepdims=True)
        ix = jnp.min(jnp.where(logits == v, eidx, float(N_EXPERTS)), axis=0, keepdims=True)
        logits = jnp.where(eidx == ix, -jnp.inf, logits)
        vals.append(v)
        idxs.append(ix)
    exps = [jnp.exp(v - vals[0]) for v in vals]
    den = exps[0] + exps[1] + exps[2] + exps[3]
    route_ref[...] = jnp.concatenate(idxs + [ex / den for ex in exps], axis=0)


def post_attention(o, x2d, p2d, wo, bo, g1, b1, wrh, wrl, br, wpg, bpg, wpp):
    n = x2d.shape[0]
    tm = min(n, 256)
    row = lambda w: pl.BlockSpec((tm, w), lambda i: (i, 0))
    return pl.pallas_call(
        _post_attn_kernel,
        out_shape=(jax.ShapeDtypeStruct((n, D_MODEL), F32), jax.ShapeDtypeStruct((n, D_MODEL), F32),
                   jax.ShapeDtypeStruct((2 * TOP_K, n), F32)),
        grid=(n // tm,),
        in_specs=[row(D_MODEL), row(D_MODEL), row(P_DIM), _full((D_MODEL, D_MODEL)), _full((1, D_MODEL)),
                  _full((1, D_MODEL)), _full((1, D_MODEL)), _full((N_EXPERTS, D_MODEL)), _full((N_EXPERTS, D_MODEL)),
                  _full((N_EXPERTS, 1)), _full((D_MODEL, D_MODEL)), _full((1, D_MODEL)), _full((P_DIM, D_MODEL))],
        out_specs=(row(D_MODEL), row(D_MODEL), pl.BlockSpec((2 * TOP_K, tm), lambda i: (0, i))),
        compiler_params=_cparams(("parallel",)),
        name="post_attention",
    )(o, x2d, p2d, wo, bo, g1, b1, wrh, wrl, br, wpg, bpg, wpp)


MOE_TILE = 256


def _route_tables(expert_idx, tm):
    k, n = expert_idx.shape
    a = k * n
    n_rows = a + N_EXPERTS * tm
    flat_e = expert_idx.reshape(a)
    order = jnp.argsort(flat_e, stable=True).astype(jnp.int32)
    cnt = jnp.sum((flat_e[:, None] == jnp.arange(N_EXPERTS, dtype=jnp.int32)[None, :]).astype(jnp.int32), axis=0)
    start = jnp.cumsum(cnt) - cnt
    pcnt = ((cnt + tm - 1) // tm) * tm
    pend = jnp.cumsum(pcnt)
    pstart = pend - pcnt
    q = jnp.arange(n_rows, dtype=jnp.int32)
    e_of_q = jnp.searchsorted(pend, q, side="right").astype(jnp.int32)
    e_c = jnp.minimum(e_of_q, N_EXPERTS - 1)
    local = q - pstart[e_c]
    valid = (e_of_q < N_EXPERTS) & (local < cnt[e_c])
    r = jnp.clip(start[e_c] + local, 0, a - 1)
    src_tok = jnp.where(valid, order[r] % n, 0).astype(jnp.int32)
    inv = jnp.argsort(order).astype(jnp.int32)
    dest = (pstart[flat_e] + inv - start[flat_e]).astype(jnp.int32)
    nt = n_rows // tm
    tile_e = e_c[::tm]
    tile_active = (jnp.arange(nt, dtype=jnp.int32) * tm < pend[-1]).astype(jnp.int32)
    prev = jnp.concatenate([jnp.full((1,), -1, jnp.int32), tile_e[:-1]])
    tile_first = ((tile_e != prev) & (tile_active > 0)).astype(jnp.int32)
    return tile_e, tile_active, tile_first, src_tok.reshape(nt, 1, tm), dest.reshape(k, n)


def _moe_kernel(te_ref, act_ref, first_ref, tok_ref, tok_next_ref,
                x_hbm, w1_ref, b1_ref, w2_ref, b2_ref, y_ref,
                xbuf, w1b, w2b, sem):
    i = pl.program_id(0)
    nt = pl.num_programs(0)
    tm = xbuf.shape[1]
    slot = i % 2

    def row_copy(tok, r, s):
        return pltpu.make_async_copy(x_hbm.at[pl.ds(tok, 1), :], xbuf.at[s, pl.ds(r, 1), :], sem.at[s])

    def issue(tref, s):
        def body(r, carry):
            row_copy(tref[0, 0, r], r, s).start()
            return carry
        lax.fori_loop(0, tm, body, 0)

    @pl.when((i == 0) & (act_ref[0] > 0))
    def _():
        issue(tok_ref, 0)

    nxt = jnp.minimum(i + 1, nt - 1)

    @pl.when((i + 1 < nt) & (act_ref[nxt] > 0))
    def _():
        issue(tok_next_ref, 1 - slot)

    @pl.when(act_ref[i] > 0)
    def _():
        def wbody(r, carry):
            row_copy(0, r, slot).wait()
            return carry
        lax.fori_loop(0, tm, wbody, 0)

        @pl.when(first_ref[i] > 0)
        def _():
            w1b[...] = w1_ref[0].astype(BF16)
            w2b[...] = w2_ref[0].astype(BF16)

        xb = xbuf[slot].astype(BF16)
        h = jnp.dot(xb, w1b[...], preferred_element_type=F32) + b1_ref[0]
        gate = jnp.minimum(h[:, :D_FF], SWIGLU_LIMIT)
        up = jnp.clip(h[:, D_FF:], -SWIGLU_LIMIT, SWIGLU_LIMIT)
        act = (up + 1.0) * gate * jax.nn.sigmoid(SWIGLU_ALPHA * gate)
        y_ref[...] = jnp.dot(act.astype(BF16), w2b[...], preferred_element_type=F32) + b2_ref[0]

    @pl.when(act_ref[i] == 0)
    def _():
        y_ref[...] = jnp.zeros_like(y_ref)


def routed_experts(x1, tile_e, tile_active, tile_first, src_tok, w_e1, b_e1, w_e2, b_e2):
    nt, _, tm = src_tok.shape
    tokspec = lambda f: pl.BlockSpec((1, 1, tm), f, memory_space=pltpu.SMEM)
    grid_spec = pltpu.PrefetchScalarGridSpec(
        num_scalar_prefetch=3,
        grid=(nt,),
        in_specs=[tokspec(lambda i, *_: (i, 0, 0)),
                  tokspec(lambda i, *_: (jnp.minimum(i + 1, nt - 1), 0, 0)),
                  pl.BlockSpec(memory_space=pl.ANY),
                  pl.BlockSpec((1, D_MODEL, 2 * D_FF), lambda i, te, *_: (te[i], 0, 0)),
                  pl.BlockSpec((1, 1, 2 * D_FF), lambda i, te, *_: (te[i], 0, 0)),
                  pl.BlockSpec((1, D_FF, D_MODEL), lambda i, te, *_: (te[i], 0, 0)),
                  pl.BlockSpec((1, 1, D_MODEL), lambda i, te, *_: (te[i], 0, 0))],
        out_specs=pl.BlockSpec((tm, D_MODEL), lambda i, *_: (i, 0)),
        scratch_shapes=[pltpu.VMEM((2, tm, D_MODEL), F32), pltpu.VMEM((D_MODEL, 2 * D_FF), BF16),
                        pltpu.VMEM((D_FF, D_MODEL), BF16), pltpu.SemaphoreType.DMA((2,))],
    )
    return pl.pallas_call(
        _moe_kernel,
        out_shape=jax.ShapeDtypeStruct((nt * tm, D_MODEL), F32),
        grid_spec=grid_spec,
        compiler_params=_cparams(("arbitrary",)),
        name="routed_experts",
    )(tile_e, tile_active, tile_first, src_tok, src_tok, x1, w_e1, b_e1.reshape(N_EXPERTS, 1, 2 * D_FF),
      w_e2, b_e2.reshape(N_EXPERTS, 1, D_MODEL))


COMBINE_TILE = 128


def _combine_kernel(pos_ref, pos_next_ref, x1_ref, e_ref, gw_ref, g2_ref, b2_ref, ys_hbm, out_ref, buf, sem):
    i = pl.program_id(0)
    nt = pl.num_programs(0)
    tm = out_ref.shape[0]
    n_rows = TOP_K * tm
    slot = i % 2

    def row_copy(src, r, s):
        return pltpu.make_async_copy(ys_hbm.at[pl.ds(src, 1), :], buf.at[s, pl.ds(r, 1), :], sem.at[s])

    def issue(pref, s):
        def body(r, carry):
            row_copy(pref[0, 0, r], r, s).start()
            return carry
        lax.fori_loop(0, n_rows, body, 0)

    @pl.when(i == 0)
    def _():
        issue(pos_ref, 0)

    @pl.when(i + 1 < nt)
    def _():
        issue(pos_next_ref, 1 - slot)

    def wbody(r, carry):
        row_copy(0, r, slot).wait()
        return carry
    lax.fori_loop(0, n_rows, wbody, 0)

    gw = gw_ref[...]
    m = jnp.zeros((tm, D_MODEL), F32)
    for k in range(TOP_K):
        m = m + gw[:, k:k + 1] * buf[slot, k * tm:(k + 1) * tm, :]
    out_ref[...] = _layer_norm(DN_ALPHA * x1_ref[...] + m + e_ref[...], g2_ref[...], b2_ref[...])


def combine_experts(pos, x1, e, gw, g2, b2, ys):
    n = x1.shape[0]
    tm = min(n, COMBINE_TILE)
    nt = n // tm
    pos_t = pos.reshape(TOP_K, nt, tm).transpose(1, 0, 2).reshape(nt, 1, TOP_K * tm)
    posspec = lambda f: pl.BlockSpec((1, 1, TOP_K * tm), f, memory_space=pltpu.SMEM)
    row = lambda w: pl.BlockSpec((tm, w), lambda i: (i, 0))
    return pl.pallas_call(
        _combine_kernel,
        out_shape=jax.ShapeDtypeStruct((n, D_MODEL), F32),
        grid=(nt,),
        in_specs=[posspec(lambda i: (i, 0, 0)), posspec(lambda i: (jnp.minimum(i + 1, nt - 1), 0, 0)),
                  row(D_MODEL), row(D_MODEL), row(TOP_K), _full((1, D_MODEL)), _full((1, D_MODEL)),
                  pl.BlockSpec(memory_space=pl.ANY)],
        out_specs=row(D_MODEL),
        scratch_shapes=[pltpu.VMEM((2, TOP_K * tm, D_MODEL), F32), pltpu.SemaphoreType.DMA((2,))],
        compiler_params=_cparams(("arbitrary",)),
        name="combine_experts",
    )(pos_t, pos_t, x1, e, gw, g2, b2, ys)


def kernel(x, p, positions, w_in, b_in, ck_pe, w_ck1, w_ck2, cv_pe, w_cv1, w_cv2, sinks, w_o, b_o,
           ln1_g, ln1_b, w_r, b_r, w_e1, b_e1, w_e2, b_e2, w_pg, b_pg, w_pp, ln2_g, ln2_b):
    batch, seq, _ = x.shape
    n = batch * seq
    n_cmp = seq // CMP_STRIDE
    inv = 1.0 / (ROPE_THETA ** (jnp.arange(0, HEAD_DIM, 2, dtype=F32) / HEAD_DIM))
    inv_tiled = jnp.tile(inv, LANES // (HEAD_DIM // 2))[None, :]
    cos, sin = rope_tables(positions.reshape(n), inv_tiled)
    pos_c = jnp.concatenate([positions[:, CMP_BLOCK - 1::CMP_STRIDE], positions[:, -1:]], axis=1)
    cos_c, sin_c = rope_tables(pos_c.reshape(batch * n_cmp), inv_tiled)
    pool = jnp.asarray(_pool_matrix(), BF16)
    gx = jnp.asarray(_gate_expand_matrix(), BF16)

    x2d = x.reshape(n, D_MODEL)
    for li in range(DEPTH):
        w_all, b_all = _prep_inproj_weights(w_in[li], b_in[li])
        qn, qs, ks, kw, kswa, kc, vc, vs, vw, vswa, gates = input_projection(x2d, w_all, b_all, cos, sin)
        kweights = _prep_compress_weights(ck_pe[li], w_ck1[li], w_ck2[li], True)
        vweights = _prep_compress_weights(cv_pe[li], w_cv1[li], w_cv2[li], False)
        kc_c, vc_c = compress_kv(kc.reshape(batch * n_cmp, CMP_STRIDE * KV_W), vc.reshape(batch * n_cmp, CMP_STRIDE * KV_W),
                                 cos_c, sin_c, kweights, vweights)
        o = attention(sinks[li], qn, qs, gates, kc_c, vc_c, ks, vs, kw, vw, kswa, vswa, pool, gx, batch, seq)

        wo = w_o[li][_WO_ROWS].astype(BF16)
        wr_t = w_r[li].T
        wrh = wr_t.astype(BF16)
        wrl = (wr_t - wrh.astype(F32)).astype(BF16)
        x1, e, route = post_attention(o, x2d, p[li].reshape(n, P_DIM), wo, b_o[li][None, :], ln1_g[li][None, :],
                                      ln1_b[li][None, :], wrh, wrl, b_r[li][:, None], w_pg[li].astype(BF16),
                                      b_pg[li][None, :], w_pp[li].astype(BF16))
        expert_idx = route[:TOP_K].astype(jnp.int32)
        gw = route[TOP_K:].T
        tile_e, tile_active, tile_first, src_tok, pos = _route_tables(expert_idx, MOE_TILE)
        ys = routed_experts(x1, tile_e, tile_active, tile_first, src_tok, w_e1[li], b_e1[li], w_e2[li], b_e2[li])
        x2d = combine_experts(pos, x1, e, gw, ln2_g[li][None, :], ln2_b[li][None, :], ys)
    return x2d.reshape(batch, seq, D_MODEL)
```

```python
import numpy as np
import jax
import jax.numpy as jnp
from jax import lax
from jax.experimental import pallas as pl
from jax.experimental.pallas import tpu as pltpu

F32 = jnp.float32
BF16 = jnp.bfloat16

D_MODEL = 1024
DEPTH = 2
HEAD_DIM = 64
Q_HEADS = 8
KV_GROUPS = 2
GROUP_HEADS = Q_HEADS // KV_GROUPS
ROPE_THETA = 10000.0
CMP_STRIDE = 16
CMP_BLOCK = 32
CMP_HIDDEN = 256
SEL_BLOCK = 64
SEL_TOPN = 8
NSA_WINDOW = 512
SWA_WINDOW = 128
N_EXPERTS = 32
TOP_K = 4
D_FF = D_MODEL
SWIGLU_LIMIT = 7.0
SWIGLU_ALPHA = 1.702
P_DIM = 256
DN_ALPHA = (2 * DEPTH) ** 0.25
LN_EPS = 1e-5
NEG = -1e30
FORCE = 1e9
LOG2E = float(np.log2(np.e))

LANES = 128
Q_CHUNK = 128
SLC_UNROLL = 4
KV_W = KV_GROUPS * HEAD_DIM
Q_W = Q_HEADS * HEAD_DIM
VMEM_LIMIT = 56 * 1024 * 1024

_OFF_QN, _OFF_KC, _OFF_VC, _OFF_KS, _OFF_VS, _OFF_KW, _OFF_VW = 0, 512, 640, 768, 896, 1024, 1152
_OFF_GATES, _OFF_QS, _OFF_KSWA, _OFF_VSWA = 1280, 1304, 1816, 1944
N_GATES = Q_HEADS * 3
GATE_ROWS = 32

N_ROPE_TILES = 11
ROPE_W = N_ROPE_TILES * LANES
PROJ_W = 2 * ROPE_W + 2 * LANES
PROJ_T_ROWS = 3 * KV_W + GATE_ROWS


def _paired_head_cols(base):
    cols = []
    for j in range(GROUP_HEADS):
        for half in range(KV_GROUPS):
            h = j + GROUP_HEADS * half
            cols.extend(base + h * HEAD_DIM + d for d in range(HEAD_DIM))
    return cols


def _proj_layout():
    rope_cols = (_paired_head_cols(_OFF_QN) + _paired_head_cols(_OFF_QS)
                 + list(range(_OFF_KS, _OFF_KS + KV_W)) + list(range(_OFF_KW, _OFF_KW + KV_W))
                 + list(range(_OFF_KSWA, _OFF_KSWA + KV_W)))
    rope_cols = np.asarray(rope_cols, np.int32)
    scale = np.ones(ROPE_W, np.float64)
    scale[:2 * Q_W] = HEAD_DIM ** -0.5 * LOG2E
    pos_in_tile = np.arange(ROPE_W) % HEAD_DIM
    first_half = pos_in_tile < HEAD_DIM // 2
    rot_cols = np.where(first_half, rope_cols + HEAD_DIM // 2, rope_cols - HEAD_DIM // 2).astype(np.int32)
    rot_sign = np.where(first_half, -1.0, 1.0)
    plain_cols = np.concatenate([np.arange(o, o + KV_W) for o in (_OFF_KC, _OFF_VC)]).astype(np.int32)
    t_cols = np.concatenate([np.arange(o, o + KV_W) for o in (_OFF_VS, _OFF_VW, _OFF_VSWA)]
                            + [np.arange(_OFF_GATES, _OFF_GATES + N_GATES)]).astype(np.int32)
    return rope_cols, scale.astype(np.float32), rot_cols, (rot_sign * scale).astype(np.float32), plain_cols, t_cols


_ROPE_COLS, _ROPE_SCALE, _ROT_COLS, _ROT_SCALE, _PLAIN_COLS, _T_COLS = _proj_layout()


def _pool_matrix():
    m = np.zeros((32, LANES), np.float32)
    for c in range(LANES - 1):
        m[c // (SEL_BLOCK // CMP_STRIDE), c] = 1.0
    return m


def _cparams(sem, vmem=VMEM_LIMIT):
    return pltpu.CompilerParams(dimension_semantics=sem, vmem_limit_bytes=vmem)


def _full(shape):
    return pl.BlockSpec(shape, lambda *_: (0,) * len(shape))


_NT = (((1,), (1,)), ((), ()))


def _rope_table_kernel(pos_ref, inv_ref, cos_ref, sin_ref):
    ang = pos_ref[...] * inv_ref[...]
    cos_ref[...] = jnp.cos(ang)
    sin_ref[...] = jnp.sin(ang)


def rope_tables(pos, inv_tiled):
    m = pos.shape[0]
    pos_b = jnp.broadcast_to(pos.astype(F32)[:, None], (m, LANES))
    tm = min(m, 1024)
    spec = pl.BlockSpec((tm, LANES), lambda i: (i, 0))
    return pl.pallas_call(
        _rope_table_kernel,
        out_shape=(jax.ShapeDtypeStruct((m, LANES), F32),) * 2,
        grid=(m // tm,),
        in_specs=[spec, _full((1, LANES))],
        out_specs=(spec, spec),
        compiler_params=_cparams(("parallel",)),
        name="rope_tables",
    )(pos_b, inv_tiled)


def _inproj_kernel(x_ref, w_ref, b_ref, wt_ref, bt_ref, cos_ref, sin_ref,
                   qn_ref, qs_ref, ks_ref, kw_ref, kswa_ref, kc_ref, vc_ref,
                   vst_ref, vwt_ref, vswat_ref, gt_ref):
    xb = x_ref[...].astype(BF16)
    cos = cos_ref[...]
    sin = sin_ref[...]

    def proj(lo, hi):
        return jnp.dot(xb, w_ref[:, lo:hi], preferred_element_type=F32) + b_ref[:, lo:hi]

    def roped(tile_lo, n_tiles, out_ref):
        lo, hi = tile_lo * LANES, (tile_lo + n_tiles) * LANES
        h = proj(lo, hi)
        hr = proj(ROPE_W + lo, ROPE_W + hi)
        for t in range(n_tiles):
            sl = slice(t * LANES, (t + 1) * LANES)
            out_ref[:, sl] = (h[:, sl] * cos + hr[:, sl] * sin).astype(out_ref.dtype)

    roped(0, 4, qn_ref)
    roped(4, 4, qs_ref)
    roped(8, 1, ks_ref)
    roped(9, 1, kw_ref)
    roped(10, 1, kswa_ref)
    plain = proj(2 * ROPE_W, PROJ_W)
    kc_ref[...] = plain[:, :LANES].astype(kc_ref.dtype)
    vc_ref[...] = plain[:, LANES:].astype(vc_ref.dtype)
    tr = lax.dot_general(wt_ref[...], xb, _NT, preferred_element_type=F32) + bt_ref[...]
    for t, ref in enumerate((vst_ref, vwt_ref, vswat_ref)):
        ref[...] = tr[t * KV_W:(t + 1) * KV_W].astype(ref.dtype)
    gt_ref[...] = tr[3 * KV_W:]


def input_projection(x2d, w_all, b_all, wt, bt, cos, sin):
    n = x2d.shape[0]
    tm = min(n, 512)
    row = lambda w: pl.BlockSpec((tm, w), lambda i: (i, 0))
    col = lambda h: pl.BlockSpec((h, tm), lambda i: (0, i))
    outs = ([jax.ShapeDtypeStruct((n, Q_W), BF16)] * 2 + [jax.ShapeDtypeStruct((n, KV_W), BF16)] * 5
            + [jax.ShapeDtypeStruct((KV_W, n), BF16)] * 3 + [jax.ShapeDtypeStruct((GATE_ROWS, n), F32)])
    return pl.pallas_call(
        _inproj_kernel,
        out_shape=tuple(outs),
        grid=(n // tm,),
        in_specs=[row(D_MODEL), _full((D_MODEL, PROJ_W)), _full((1, PROJ_W)), _full((PROJ_T_ROWS, D_MODEL)),
                  _full((PROJ_T_ROWS, 1)), row(LANES), row(LANES)],
        out_specs=tuple([row(Q_W)] * 2 + [row(KV_W)] * 5 + [col(KV_W)] * 3 + [col(GATE_ROWS)]),
        compiler_params=_cparams(("parallel",)),
        name="input_projection",
    )(x2d, w_all, b_all, wt, bt, cos, sin)


def _prep_inproj_weights(w_in, b_in):
    w = jnp.concatenate([w_in[:, _ROPE_COLS] * _ROPE_SCALE, w_in[:, _ROT_COLS] * _ROT_SCALE, w_in[:, _PLAIN_COLS]], axis=1)
    b = jnp.concatenate([b_in[_ROPE_COLS] * _ROPE_SCALE, b_in[_ROT_COLS] * _ROT_SCALE, b_in[_PLAIN_COLS]])
    pad = GATE_ROWS - N_GATES
    wt = jnp.concatenate([w_in[:, _T_COLS].T, jnp.zeros((pad, D_MODEL), F32)], axis=0)
    bt = jnp.concatenate([b_in[_T_COLS], jnp.zeros((pad,), F32)])
    return w.astype(BF16), b[None, :], wt.astype(BF16), bt[:, None]


def _gelu_tanh(x):
    return 0.5 * x * (1.0 + jnp.tanh(np.sqrt(2.0 / np.pi) * (x + 0.044715 * (x * x * x))))


def _compress_kernel(kc_ref, vc_ref, cos_ref, sin_ref,
                     kw1a_ref, kw1b_ref, kpe_ref, kw2_ref, kw2r_ref,
                     vw1a_ref, vw1b_ref, vpe_ref, vw2t_ref,
                     ko_ref, vot_ref):
    def hidden(x_ref, w1a_ref, w1b_ref, pe_ref):
        x = x_ref[...]
        ya = jnp.dot(x, w1a_ref[...], preferred_element_type=F32)
        yb = jnp.dot(x, w1b_ref[...], preferred_element_type=F32)
        rows = ya.shape[0]
        h = ya + pltpu.roll(yb, rows - 1, 0)
        pe = pe_ref[...]
        peb = (jnp.dot(pe[:, :CMP_STRIDE * KV_W], w1a_ref[...], preferred_element_type=F32)
               + jnp.dot(pe[:, CMP_STRIDE * KV_W:], w1b_ref[...], preferred_element_type=F32))
        return _gelu_tanh(h + peb[0:1, :]).astype(BF16)

    ak = hidden(kc_ref, kw1a_ref, kw1b_ref, kpe_ref)
    kc = jnp.dot(ak, kw2_ref[...], preferred_element_type=F32)
    kcr = jnp.dot(ak, kw2r_ref[...], preferred_element_type=F32)
    ko_ref[...] = (kc * cos_ref[...] + kcr * sin_ref[...]).astype(ko_ref.dtype)
    av = hidden(vc_ref, vw1a_ref, vw1b_ref, vpe_ref)
    vot_ref[...] = lax.dot_general(vw2t_ref[...], av, _NT, preferred_element_type=F32).astype(vot_ref.dtype)


def _prep_compress_weights(pe, w1, w2, is_key):
    eye = jnp.eye(KV_GROUPS, dtype=F32)
    w1r = w1.reshape(CMP_BLOCK, HEAD_DIM, CMP_HIDDEN)

    def half(wh):
        return jnp.einsum("idh,ge->igdeh", wh, eye).reshape(CMP_STRIDE * KV_W, KV_GROUPS * CMP_HIDDEN).astype(BF16)

    def block_diag(w):
        wb = jnp.einsum("hd,ge->ghed", w, eye)
        return wb.reshape(KV_GROUPS * CMP_HIDDEN, KV_W)

    w1a, w1b = half(w1r[:CMP_STRIDE]), half(w1r[CMP_STRIDE:])
    pe_row = jnp.broadcast_to(pe[:, None, :], (CMP_BLOCK, KV_GROUPS, HEAD_DIM)).reshape(1, CMP_BLOCK * KV_W)
    pe_rows = jnp.broadcast_to(pe_row, (8, CMP_BLOCK * KV_W)).astype(BF16)
    if not is_key:
        return [w1a, w1b, pe_rows, block_diag(w2).T.astype(BF16)]
    half_d = HEAD_DIM // 2
    w2rot = jnp.concatenate([-w2[:, half_d:], w2[:, :half_d]], axis=1)
    return [w1a, w1b, pe_rows, block_diag(w2).astype(BF16), block_diag(w2rot).astype(BF16)]


def compress_kv(kc_chunks, vc_chunks, cos_c, sin_c, kweights, vweights):
    rows = kc_chunks.shape[0]
    tm = min(rows, 512)
    cw = CMP_STRIDE * KV_W
    hw = KV_GROUPS * CMP_HIDDEN
    row = lambda w: pl.BlockSpec((tm, w), lambda i: (i, 0))
    wspecs_k = [_full((cw, hw)), _full((cw, hw)), _full((8, 2 * cw)), _full((hw, KV_W)), _full((hw, KV_W))]
    wspecs_v = wspecs_k[:3] + [_full((KV_W, hw))]
    return pl.pallas_call(
        _compress_kernel,
        out_shape=(jax.ShapeDtypeStruct((rows, KV_W), BF16), jax.ShapeDtypeStruct((KV_W, rows), BF16)),
        grid=(rows // tm,),
        in_specs=[row(cw), row(cw), row(LANES), row(LANES)] + wspecs_k + wspecs_v,
        out_specs=(row(KV_W), pl.BlockSpec((KV_W, tm), lambda i: (0, i))),
        compiler_params=_cparams(("parallel",)),
        name="compress_kv",
    )(kc_chunks, vc_chunks, cos_c, sin_c, *kweights, *vweights)


def _attn_kernel(sinks_ref, qn_ref, qs_ref, gt_ref, kc_ref, vct_ref,
                 ks_ref, vst_ref, kw_ref, vwt_ref, ksw_ref, vswt_ref, pool_ref, o_ref,
                 sel_ref, sc_slc, sc_win, sc_swa):
    i = pl.program_id(1)
    c = Q_CHUNK
    cols4 = GROUP_HEADS * c
    key = lax.broadcasted_iota(jnp.int32, (c, c), 0)
    tq = lax.broadcasted_iota(jnp.int32, (c, c), 1)
    t_minus_key = tq - key
    lane = lax.broadcasted_iota(jnp.int32, (c, LANES), 1)
    group_mask = [jnp.where(lane < HEAD_DIM, 1.0, 0.0).astype(BF16), jnp.where(lane < HEAD_DIM, 0.0, 1.0).astype(BF16)]

    def rep(a):
        return jnp.concatenate([a] * GROUP_HEADS, axis=1)

    def stack_q(q_ref, g):
        return jnp.concatenate([q_ref[:, j * LANES:(j + 1) * LANES] * group_mask[g] for j in range(GROUP_HEADS)], axis=0)

    sub = c // 8

    def part_max(x):
        return jnp.max(x.reshape(sub, 8, x.shape[-1]), axis=0)

    def part_sum(x):
        return jnp.sum(x.reshape(sub, 8, x.shape[-1]), axis=0)

    def score_block(q, k_blk, mask, sc_ref, row0, mpart):
        s = lax.dot_general(k_blk, q, _NT, preferred_element_type=F32)
        s = jnp.where(rep(mask), s, NEG)
        sc_ref[pl.ds(row0, c), :] = s
        return jnp.maximum(mpart, part_max(s))

    def prob_block(vt_blk, sc_ref, row0, m, carry):
        lpart, acc = carry
        p = jnp.exp2(sc_ref[pl.ds(row0, c), :] - m)
        return lpart + part_sum(p), acc + jnp.dot(vt_blk, p.astype(BF16), preferred_element_type=F32)

    mpart0 = jnp.full((8, cols4), NEG, F32)
    carry0 = (jnp.zeros((8, cols4), F32), jnp.zeros((KV_W, cols4), F32))

    def banded(q, k_ref, vt_ref, window, sc_ref, sink=None):
        nblk = _band_blocks(window)
        offs = []
        mpart = mpart0
        for j in range(nblk):
            back = nblk - 1 - j
            kb = i - back
            off = pl.multiple_of(jnp.maximum(kb, 0) * c, c)
            offs.append(off)
            width = jnp.where(kb >= 0, window, 0).astype(jnp.uint32)
            mask = (t_minus_key + back * c).astype(jnp.uint32) < width
            mpart = score_block(q, k_ref[pl.ds(off, c), :], mask, sc_ref, j * c, mpart)
        m = jnp.max(mpart, axis=0, keepdims=True)
        if sink is not None:
            m = jnp.maximum(m, sink)
        carry = carry0
        for j in range(nblk):
            carry = prob_block(vt_ref[:, pl.ds(offs[j], c)], sc_ref, j * c, m, carry)
        l = jnp.sum(carry[0], axis=0, keepdims=True)
        if sink is not None:
            l = l + jnp.exp2(sink - m)
        return carry[1] * (1.0 / jnp.maximum(l, 1e-30))

    def compressed(q):
        s = lax.dot_general(kc_ref[...], q, _NT, preferred_element_type=F32)
        cend = key * CMP_STRIDE + (CMP_BLOCK - 1)
        mk = rep(cend <= tq + i * c)
        s = jnp.where(mk, s, NEG)
        m = jnp.max(s, axis=0, keepdims=True)
        e = jnp.where(mk, jnp.exp2(s - m), 0.0)
        p = e * (1.0 / jnp.maximum(jnp.sum(e, axis=0, keepdims=True), 1e-30))
        o = jnp.dot(vct_ref[...], p.astype(BF16), preferred_element_type=F32)
        psum = p[:, 0:c] + p[:, c:2 * c] + p[:, 2 * c:3 * c] + p[:, 3 * c:4 * c]
        return o, psum

    nsel = pool_ref.shape[0]
    blk = lax.broadcasted_iota(jnp.int32, (nsel, c), 0)
    tcol = lax.broadcasted_iota(jnp.int32, (nsel, c), 1)

    def select_blocks(psum):
        hi = psum.astype(BF16)
        lo = (psum - hi.astype(F32)).astype(BF16)
        pool = pool_ref[...]
        imp = jnp.dot(pool, hi, preferred_element_type=F32) + jnp.dot(pool, lo, preferred_element_type=F32)
        t = tcol + i * c
        cur = t >> 6
        forced = (blk == 0) | (blk == cur) | (blk == cur - 1)
        valid = (blk << 6) <= t
        score = jnp.where(forced, FORCE, jnp.where(valid, imp, -1.0))
        rank = jnp.zeros((nsel, c), F32)
        for k in range(nsel):
            sk = score[k:k + 1, :]
            tie = jnp.where(blk > k, 1.0, 0.0)
            rank = rank + jnp.where(sk > score, 1.0, jnp.where(sk == score, tie, 0.0))
        sel_ref[...] = jnp.where(rank < SEL_TOPN, 1.0, 0.0)

    def selected(q, sc_ref):
        nquad = (i + SLC_UNROLL) // SLC_UNROLL

        def block_mask(kb):
            first = sel_ref[pl.ds(2 * kb, 1), :]
            second = sel_ref[pl.ds(2 * kb + 1, 1), :]
            chosen = jnp.where(key < SEL_BLOCK, first, second) > 0.5
            return chosen & (t_minus_key + (i - kb) * c >= 0)

        def phase_a(qd, mpart):
            for u in range(SLC_UNROLL):
                kb = qd * SLC_UNROLL + u
                off = pl.multiple_of(kb * c, c)
                mpart = score_block(q, ks_ref[pl.ds(off, c), :], block_mask(kb), sc_ref, off, mpart)
            return mpart

        m = jnp.max(lax.fori_loop(0, nquad, phase_a, mpart0), axis=0, keepdims=True)

        def phase_b(qd, carry):
            for u in range(SLC_UNROLL):
                off = pl.multiple_of((qd * SLC_UNROLL + u) * c, c)
                carry = prob_block(vst_ref[:, pl.ds(off, c)], sc_ref, off, m, carry)
            return carry

        lpart, acc = lax.fori_loop(0, nquad, phase_b, carry0)
        return acc * (1.0 / jnp.maximum(jnp.sum(lpart, axis=0, keepdims=True), 1e-30))

    gs = jax.nn.sigmoid(gt_ref[...])
    heads = []
    for g in range(KV_GROUPS):
        q = stack_q(qn_ref, g)
        o_cmp, psum = compressed(q)
        select_blocks(psum)
        o_slc = selected(q, sc_slc)
        o_win = banded(q, kw_ref, vwt_ref, NSA_WINDOW, sc_win)
        rows = slice(g * HEAD_DIM, (g + 1) * HEAD_DIM)
        for r in range(GROUP_HEADS):
            h = g * GROUP_HEADS + r
            cs = slice(r * c, (r + 1) * c)
            heads.append(gs[3 * h:3 * h + 1] * o_cmp[rows, cs] + gs[3 * h + 1:3 * h + 2] * o_slc[rows, cs]
                         + gs[3 * h + 2:3 * h + 3] * o_win[rows, cs])

    head_col = lax.broadcasted_iota(jnp.int32, (1, cols4), 1) // c
    for g in range(KV_GROUPS):
        sink = jnp.zeros((1, cols4), F32)
        for r in range(GROUP_HEADS):
            sink = jnp.where(head_col == r, sinks_ref[g * GROUP_HEADS + r] * LOG2E, sink)
        o_swa = banded(stack_q(qs_ref, g), ksw_ref, vswt_ref, SWA_WINDOW, sc_swa, sink)
        for r in range(GROUP_HEADS):
            heads.append(o_swa[g * HEAD_DIM:(g + 1) * HEAD_DIM, r * c:(r + 1) * c])
    o_ref[...] = jnp.concatenate(heads, axis=0).T.astype(o_ref.dtype)


def _band_blocks(window):
    return (window - 1 + Q_CHUNK - 1) // Q_CHUNK + 1


def attention(sinks, qn, qs, gates_t, kc_c, vc_ct, ks, vst, kw, vwt, ksw, vswt, pool, batch, seq):
    n = batch * seq
    nq = seq // Q_CHUNK
    assert seq // SEL_BLOCK == pool.shape[0] and seq // CMP_STRIDE == LANES
    qspec = lambda w: pl.BlockSpec((Q_CHUNK, w), lambda b, i, *_: (b * nq + i, 0))
    kspec = pl.BlockSpec((seq, KV_W), lambda b, i, *_: (b, 0))
    vtspec = pl.BlockSpec((KV_W, seq), lambda b, i, *_: (0, b))
    grid_spec = pltpu.PrefetchScalarGridSpec(
        num_scalar_prefetch=1,
        grid=(batch, nq),
        in_specs=[qspec(Q_W), qspec(Q_W), pl.BlockSpec((GATE_ROWS, Q_CHUNK), lambda b, i, *_: (0, b * nq + i)),
                  pl.BlockSpec((LANES, KV_W), lambda b, i, *_: (b, 0)), pl.BlockSpec((KV_W, LANES), lambda b, i, *_: (0, b)),
                  kspec, vtspec, kspec, vtspec, kspec, vtspec, _full(pool.shape)],
        out_specs=qspec(2 * Q_W),
        scratch_shapes=[pltpu.VMEM(pool.shape[:1] + (Q_CHUNK,), F32),
                        pltpu.VMEM((seq, GROUP_HEADS * Q_CHUNK), F32),
                        pltpu.VMEM((_band_blocks(NSA_WINDOW) * Q_CHUNK, GROUP_HEADS * Q_CHUNK), F32),
                        pltpu.VMEM((_band_blocks(SWA_WINDOW) * Q_CHUNK, GROUP_HEADS * Q_CHUNK), F32)],
    )
    return pl.pallas_call(
        _attn_kernel,
        out_shape=jax.ShapeDtypeStruct((n, 2 * Q_W), BF16),
        grid_spec=grid_spec,
        compiler_params=_cparams(("parallel", "arbitrary")),
        name="attention",
    )(sinks, qn, qs, gates_t, kc_c, vc_ct, ks, vst, kw, vwt, ksw, vswt, pool)


ROW_SLABS = D_MODEL // LANES


def _store_row_slabs(ref, val):
    for s in range(ROW_SLABS):
        ref[:, s, :] = val[:, s * LANES:(s + 1) * LANES]


def _load_row_slabs(ref):
    return jnp.concatenate([ref[:, s, :] for s in range(ROW_SLABS)], axis=1)


def _layer_norm(y, g, b):
    mu = jnp.mean(y, axis=-1, keepdims=True)
    yc = y - mu
    var = jnp.mean(yc * yc, axis=-1, keepdims=True)
    return yc * lax.rsqrt(var + LN_EPS) * g + b


def _post_attn_kernel(o_ref, x_ref, p_ref, wo_ref, bo_ref, g1_ref, b1_ref,
                      wrh_ref, wrl_ref, br_ref, wpg_ref, bpg_ref, wpp_ref,
                      x1_ref, e_ref, route_ref):
    a = jnp.dot(o_ref[...], wo_ref[...], preferred_element_type=F32) + bo_ref[...]
    x1 = _layer_norm(DN_ALPHA * x_ref[...] + a, g1_ref[...], b1_ref[...])
    _store_row_slabs(x1_ref, x1)
    xh = x1.astype(BF16)
    gate = jax.nn.sigmoid(jnp.dot(xh, wpg_ref[...], preferred_element_type=F32) + bpg_ref[...])
    e_ref[...] = gate * jnp.dot(p_ref[...].astype(BF16), wpp_ref[...], preferred_element_type=F32)
    xl = (x1 - xh.astype(F32)).astype(BF16)
    logits = (lax.dot_general(wrh_ref[...], xh, _NT, preferred_element_type=F32)
              + lax.dot_general(wrh_ref[...], xl, _NT, preferred_element_type=F32)
              + lax.dot_general(wrl_ref[...], xh, _NT, preferred_element_type=F32)) + br_ref[...]
    eidx = lax.broadcasted_iota(jnp.int32, logits.shape, 0).astype(F32)
    vals, idxs = [], []
    for _ in range(TOP_K):
        v = jnp.max(logits, axis=0, keepdims=True)
        ix = jnp.min(jnp.where(logits == v, eidx, float(N_EXPERTS)), axis=0, keepdims=True)
        logits = jnp.where(eidx == ix, -jnp.inf, logits)
        vals.append(v)
        idxs.append(ix)
    exps = [jnp.exp(v - vals[0]) for v in vals]
    den = exps[0] + exps[1] + exps[2] + exps[3]
    route_ref[...] = jnp.concatenate(idxs + [ex / den for ex in exps], axis=0)


def post_attention(o, x2d, p2d, wo, bo, g1, b1, wrh, wrl, br, wpg, bpg, wpp):
    n = x2d.shape[0]
    tm = min(n, 256)
    row = lambda w: pl.BlockSpec((tm, w), lambda i: (i, 0))
    return pl.pallas_call(
        _post_attn_kernel,
        out_shape=(jax.ShapeDtypeStruct((n, ROW_SLABS, LANES), F32), jax.ShapeDtypeStruct((n, D_MODEL), F32),
                   jax.ShapeDtypeStruct((2 * TOP_K, n), F32)),
        grid=(n // tm,),
        in_specs=[row(D_MODEL), row(D_MODEL), row(P_DIM), _full((D_MODEL, D_MODEL)), _full((1, D_MODEL)),
                  _full((1, D_MODEL)), _full((1, D_MODEL)), _full((N_EXPERTS, D_MODEL)), _full((N_EXPERTS, D_MODEL)),
                  _full((N_EXPERTS, 1)), _full((D_MODEL, D_MODEL)), _full((1, D_MODEL)), _full((P_DIM, D_MODEL))],
        out_specs=(pl.BlockSpec((tm, ROW_SLABS, LANES), lambda i: (i, 0, 0)), row(D_MODEL),
                   pl.BlockSpec((2 * TOP_K, tm), lambda i: (0, i))),
        compiler_params=_cparams(("parallel",)),
        name="post_attention",
    )(o, x2d, p2d, wo, bo, g1, b1, wrh, wrl, br, wpg, bpg, wpp)


MOE_TILE = 256
DMA_ISSUE_UNROLL = 8


def _route_tables(expert_idx, tm):
    k, n = expert_idx.shape
    a = k * n
    n_rows = a + N_EXPERTS * tm
    flat_e = expert_idx.reshape(a)
    order = jnp.argsort(flat_e, stable=True).astype(jnp.int32)
    cnt = jnp.sum((flat_e[:, None] == jnp.arange(N_EXPERTS, dtype=jnp.int32)[None, :]).astype(jnp.int32), axis=0)
    start = jnp.cumsum(cnt) - cnt
    pcnt = ((cnt + tm - 1) // tm) * tm
    pend = jnp.cumsum(pcnt)
    pstart = pend - pcnt
    q = jnp.arange(n_rows, dtype=jnp.int32)
    e_of_q = jnp.sum((q[:, None] >= pend[None, :]).astype(jnp.int32), axis=1)
    e_c = jnp.minimum(e_of_q, N_EXPERTS - 1)
    local = q - pstart[e_c]
    valid = (e_of_q < N_EXPERTS) & (local < cnt[e_c])
    r = jnp.clip(start[e_c] + local, 0, a - 1)
    src_tok = jnp.where(valid, order[r] % n, 0).astype(jnp.int32)
    inv = jnp.argsort(order).astype(jnp.int32)
    dest = (pstart[flat_e] + inv - start[flat_e]).astype(jnp.int32)
    nt = n_rows // tm
    tile_e = e_c[::tm]
    tile_active = (jnp.arange(nt, dtype=jnp.int32) * tm < pend[-1]).astype(jnp.int32)
    prev = jnp.concatenate([jnp.full((1,), -1, jnp.int32), tile_e[:-1]])
    tile_first = ((tile_e != prev) & (tile_active > 0)).astype(jnp.int32)
    return tile_e, tile_active, tile_first, src_tok.reshape(nt, 1, tm), dest.reshape(k, n)


def _moe_kernel(te_ref, act_ref, first_ref, tok_ref, tok_next_ref,
                x_hbm, w1_ref, b1_ref, w2_ref, b2_ref, y_ref,
                xbuf, w1b, w2b, sem):
    i = pl.program_id(0)
    nt = pl.num_programs(0)
    tm = xbuf.shape[1]
    slot = i % 2

    def issue(tref, s):
        def body(r, carry):
            pltpu.make_async_copy(x_hbm.at[tref[0, 0, r]], xbuf.at[s, r], sem.at[s]).start()
            return carry
        lax.fori_loop(0, tm, body, 0, unroll=DMA_ISSUE_UNROLL)

    @pl.when((i == 0) & (act_ref[0] > 0))
    def _():
        issue(tok_ref, 0)

    nxt = jnp.minimum(i + 1, nt - 1)

    @pl.when((i + 1 < nt) & (act_ref[nxt] > 0))
    def _():
        issue(tok_next_ref, 1 - slot)

    @pl.when(act_ref[i] > 0)
    def _():
        pltpu.make_async_copy(x_hbm.at[pl.ds(0, tm)], xbuf.at[slot], sem.at[slot]).wait()

        @pl.when(first_ref[i] > 0)
        def _():
            w1b[...] = w1_ref[0].astype(BF16)
            w2b[...] = w2_ref[0].astype(BF16)

        xb = _load_row_slabs(xbuf.at[slot]).astype(BF16)
        h = jnp.dot(xb, w1b[...], preferred_element_type=F32) + b1_ref[0]
        gate = jnp.minimum(h[:, :D_FF], SWIGLU_LIMIT)
        up = jnp.clip(h[:, D_FF:], -SWIGLU_LIMIT, SWIGLU_LIMIT)
        act = (up + 1.0) * gate * jax.nn.sigmoid(SWIGLU_ALPHA * gate)
        _store_row_slabs(y_ref, jnp.dot(act.astype(BF16), w2b[...], preferred_element_type=F32) + b2_ref[0])

    @pl.when(act_ref[i] == 0)
    def _():
        y_ref[...] = jnp.zeros_like(y_ref)


def routed_experts(x1, tile_e, tile_active, tile_first, src_tok, w_e1, b_e1, w_e2, b_e2):
    nt, _, tm = src_tok.shape
    tokspec = lambda f: pl.BlockSpec((1, 1, tm), f, memory_space=pltpu.SMEM)
    grid_spec = pltpu.PrefetchScalarGridSpec(
        num_scalar_prefetch=3,
        grid=(nt,),
        in_specs=[tokspec(lambda i, *_: (i, 0, 0)),
                  tokspec(lambda i, *_: (jnp.minimum(i + 1, nt - 1), 0, 0)),
                  pl.BlockSpec(memory_space=pl.ANY),
                  pl.BlockSpec((1, D_MODEL, 2 * D_FF), lambda i, te, *_: (te[i], 0, 0)),
                  pl.BlockSpec((1, 1, 2 * D_FF), lambda i, te, *_: (te[i], 0, 0)),
                  pl.BlockSpec((1, D_FF, D_MODEL), lambda i, te, *_: (te[i], 0, 0)),
                  pl.BlockSpec((1, 1, D_MODEL), lambda i, te, *_: (te[i], 0, 0))],
        out_specs=pl.BlockSpec((tm, ROW_SLABS, LANES), lambda i, *_: (i, 0, 0)),
        scratch_shapes=[pltpu.VMEM((2, tm, ROW_SLABS, LANES), F32), pltpu.VMEM((D_MODEL, 2 * D_FF), BF16),
                        pltpu.VMEM((D_FF, D_MODEL), BF16), pltpu.SemaphoreType.DMA((2,))],
    )
    return pl.pallas_call(
        _moe_kernel,
        out_shape=jax.ShapeDtypeStruct((nt * tm, ROW_SLABS, LANES), F32),
        grid_spec=grid_spec,
        compiler_params=_cparams(("arbitrary",)),
        name="routed_experts",
    )(tile_e, tile_active, tile_first, src_tok, src_tok, x1, w_e1, b_e1.reshape(N_EXPERTS, 1, 2 * D_FF),
      w_e2, b_e2.reshape(N_EXPERTS, 1, D_MODEL))


COMBINE_TILE = 128


def _combine_kernel(pos_ref, pos_next_ref, x1_ref, e_ref, gw_ref, g2_ref, b2_ref, ys_hbm, out_ref, buf, sem):
    i = pl.program_id(0)
    nt = pl.num_programs(0)
    tm = out_ref.shape[0]
    n_rows = TOP_K * tm
    slot = i % 2

    def issue(pref, s):
        def body(r, carry):
            pltpu.make_async_copy(ys_hbm.at[pref[0, 0, r]], buf.at[s, r], sem.at[s]).start()
            return carry
        lax.fori_loop(0, n_rows, body, 0, unroll=DMA_ISSUE_UNROLL)

    @pl.when(i == 0)
    def _():
        issue(pos_ref, 0)

    @pl.when(i + 1 < nt)
    def _():
        issue(pos_next_ref, 1 - slot)

    pltpu.make_async_copy(ys_hbm.at[pl.ds(0, n_rows)], buf.at[slot], sem.at[slot]).wait()

    gw = gw_ref[...]
    m = jnp.zeros((tm, D_MODEL), F32)
    for k in range(TOP_K):
        m = m + gw[:, k:k + 1] * _load_row_slabs(buf.at[slot, pl.ds(k * tm, tm)])
    out_ref[...] = _layer_norm(DN_ALPHA * _load_row_slabs(x1_ref) + m + e_ref[...], g2_ref[...], b2_ref[...])


def combine_experts(pos, x1, e, gw, g2, b2, ys):
    n = x1.shape[0]
    tm = min(n, COMBINE_TILE)
    nt = n // tm
    pos_t = pos.reshape(TOP_K, nt, tm).transpose(1, 0, 2).reshape(nt, 1, TOP_K * tm)
    posspec = lambda f: pl.BlockSpec((1, 1, TOP_K * tm), f, memory_space=pltpu.SMEM)
    row = lambda w: pl.BlockSpec((tm, w), lambda i: (i, 0))
    return pl.pallas_call(
        _combine_kernel,
        out_shape=jax.ShapeDtypeStruct((n, D_MODEL), F32),
        grid=(nt,),
        in_specs=[posspec(lambda i: (i, 0, 0)), posspec(lambda i: (jnp.minimum(i + 1, nt - 1), 0, 0)),
                  pl.BlockSpec((tm, ROW_SLABS, LANES), lambda i: (i, 0, 0)), row(D_MODEL), row(TOP_K),
                  _full((1, D_MODEL)), _full((1, D_MODEL)), pl.BlockSpec(memory_space=pl.ANY)],
        out_specs=row(D_MODEL),
        scratch_shapes=[pltpu.VMEM((2, TOP_K * tm, ROW_SLABS, LANES), F32), pltpu.SemaphoreType.DMA((2,))],
        compiler_params=_cparams(("arbitrary",)),
        name="combine_experts",
    )(pos_t, pos_t, x1, e, gw, g2, b2, ys)


def kernel(x, p, positions, w_in, b_in, ck_pe, w_ck1, w_ck2, cv_pe, w_cv1, w_cv2, sinks, w_o, b_o,
           ln1_g, ln1_b, w_r, b_r, w_e1, b_e1, w_e2, b_e2, w_pg, b_pg, w_pp, ln2_g, ln2_b):
    batch, seq, _ = x.shape
    n = batch * seq
    n_cmp = seq // CMP_STRIDE
    inv = 1.0 / (ROPE_THETA ** (jnp.arange(0, HEAD_DIM, 2, dtype=F32) / HEAD_DIM))
    inv_tiled = jnp.tile(inv, LANES // (HEAD_DIM // 2))[None, :]
    cos, sin = rope_tables(positions.reshape(n), inv_tiled)
    pos_c = jnp.concatenate([positions[:, CMP_BLOCK - 1::CMP_STRIDE], positions[:, -1:]], axis=1)
    cos_c, sin_c = rope_tables(pos_c.reshape(batch * n_cmp), inv_tiled)
    pool = jnp.asarray(_pool_matrix(), BF16)

    x2d = x.reshape(n, D_MODEL)
    for li in range(DEPTH):
        qn, qs, ks, kw, kswa, kc, vc, vst, vwt, vswat, gates_t = input_projection(
            x2d, *_prep_inproj_weights(w_in[li], b_in[li]), cos, sin)
        kweights = _prep_compress_weights(ck_pe[li], w_ck1[li], w_ck2[li], True)
        vweights = _prep_compress_weights(cv_pe[li], w_cv1[li], w_cv2[li], False)
        kc_c, vc_ct = compress_kv(kc.reshape(batch * n_cmp, CMP_STRIDE * KV_W), vc.reshape(batch * n_cmp, CMP_STRIDE * KV_W),
                                  cos_c, sin_c, kweights, vweights)
        o = attention(sinks[li], qn, qs, gates_t, kc_c, vc_ct, ks, vst, kw, vwt, kswa, vswat, pool, batch, seq)

        wr_t = w_r[li].T
        wrh = wr_t.astype(BF16)
        wrl = (wr_t - wrh.astype(F32)).astype(BF16)
        x1, e, route = post_attention(o, x2d, p[li].reshape(n, P_DIM), w_o[li].astype(BF16), b_o[li][None, :],
                                      ln1_g[li][None, :], ln1_b[li][None, :], wrh, wrl, b_r[li][:, None],
                                      w_pg[li].astype(BF16), b_pg[li][None, :], w_pp[li].astype(BF16))
        expert_idx = route[:TOP_K].astype(jnp.int32)
        gw = route[TOP_K:].T
        tile_e, tile_active, tile_first, src_tok, pos = _route_tables(expert_idx, MOE_TILE)
        ys = routed_experts(x1, tile_e, tile_active, tile_first, src_tok, w_e1[li], b_e1[li], w_e2[li], b_e2[li])
        x2d = combine_experts(pos, x1, e, gw, ln2_g[li][None, :], ln2_b[li][None, :], ys)
    return x2d.reshape(batch, seq, D_MODEL)
```

```python
import numpy as np
import jax
import jax.numpy as jnp
from jax import lax
from jax.experimental import pallas as pl
from jax.experimental.pallas import tpu as pltpu

F32 = jnp.float32
BF16 = jnp.bfloat16

D_MODEL = 1024
DEPTH = 2
HEAD_DIM = 64
Q_HEADS = 8
KV_GROUPS = 2
GROUP_HEADS = Q_HEADS // KV_GROUPS
ROPE_THETA = 10000.0
CMP_STRIDE = 16
CMP_BLOCK = 32
CMP_HIDDEN = 256
SEL_BLOCK = 64
SEL_TOPN = 8
NSA_WINDOW = 512
SWA_WINDOW = 128
N_EXPERTS = 32
TOP_K = 4
D_FF = D_MODEL
SWIGLU_LIMIT = 7.0
SWIGLU_ALPHA = 1.702
P_DIM = 256
DN_ALPHA = (2 * DEPTH) ** 0.25
LN_EPS = 1e-5
NEG = -1e30
FORCE = 1e9
LOG2E = float(np.log2(np.e))

LANES = 128
Q_CHUNK = 128
SLC_UNROLL = 4
KV_W = KV_GROUPS * HEAD_DIM
Q_W = Q_HEADS * HEAD_DIM
VMEM_LIMIT = 56 * 1024 * 1024

_OFF_QN, _OFF_KC, _OFF_VC, _OFF_KS, _OFF_VS, _OFF_KW, _OFF_VW = 0, 512, 640, 768, 896, 1024, 1152
_OFF_GATES, _OFF_QS, _OFF_KSWA, _OFF_VSWA = 1280, 1304, 1816, 1944
N_GATES = Q_HEADS * 3
GATE_ROWS = 32

N_ROPE_TILES = 11
ROPE_W = N_ROPE_TILES * LANES
PROJ_W = 2 * ROPE_W + 2 * LANES
PROJ_T_ROWS = 3 * KV_W + GATE_ROWS


def _paired_head_cols(base):
    cols = []
    for j in range(GROUP_HEADS):
        for half in range(KV_GROUPS):
            h = j + GROUP_HEADS * half
            cols.extend(base + h * HEAD_DIM + d for d in range(HEAD_DIM))
    return cols


def _proj_layout():
    rope_cols = (_paired_head_cols(_OFF_QN) + _paired_head_cols(_OFF_QS)
                 + list(range(_OFF_KS, _OFF_KS + KV_W)) + list(range(_OFF_KW, _OFF_KW + KV_W))
                 + list(range(_OFF_KSWA, _OFF_KSWA + KV_W)))
    rope_cols = np.asarray(rope_cols, np.int32)
    scale = np.ones(ROPE_W, np.float64)
    scale[:2 * Q_W] = HEAD_DIM ** -0.5 * LOG2E
    pos_in_tile = np.arange(ROPE_W) % HEAD_DIM
    first_half = pos_in_tile < HEAD_DIM // 2
    rot_cols = np.where(first_half, rope_cols + HEAD_DIM // 2, rope_cols - HEAD_DIM // 2).astype(np.int32)
    rot_sign = np.where(first_half, -1.0, 1.0)
    plain_cols = np.concatenate([np.arange(o, o + KV_W) for o in (_OFF_KC, _OFF_VC)]).astype(np.int32)
    t_cols = np.concatenate([np.arange(o, o + KV_W) for o in (_OFF_VS, _OFF_VW, _OFF_VSWA)]
                            + [np.arange(_OFF_GATES, _OFF_GATES + N_GATES)]).astype(np.int32)
    return rope_cols, scale.astype(np.float32), rot_cols, (rot_sign * scale).astype(np.float32), plain_cols, t_cols


_ROPE_COLS, _ROPE_SCALE, _ROT_COLS, _ROT_SCALE, _PLAIN_COLS, _T_COLS = _proj_layout()


def _pool_matrix():
    m = np.zeros((32, LANES), np.float32)
    for c in range(LANES - 1):
        m[c // (SEL_BLOCK // CMP_STRIDE), c] = 1.0
    return m


def _cparams(sem, vmem=VMEM_LIMIT):
    return pltpu.CompilerParams(dimension_semantics=sem, vmem_limit_bytes=vmem)


def _full(shape):
    return pl.BlockSpec(shape, lambda *_: (0,) * len(shape))


_NT = (((1,), (1,)), ((), ()))


def _rope_table_kernel(pos_ref, inv_ref, cos_ref, sin_ref):
    ang = pos_ref[...] * inv_ref[...]
    cos_ref[...] = jnp.cos(ang)
    sin_ref[...] = jnp.sin(ang)


def rope_tables(pos, inv_tiled):
    m = pos.shape[0]
    pos_b = jnp.broadcast_to(pos.astype(F32)[:, None], (m, LANES))
    tm = min(m, 1024)
    spec = pl.BlockSpec((tm, LANES), lambda i: (i, 0))
    return pl.pallas_call(
        _rope_table_kernel,
        out_shape=(jax.ShapeDtypeStruct((m, LANES), F32),) * 2,
        grid=(m // tm,),
        in_specs=[spec, _full((1, LANES))],
        out_specs=(spec, spec),
        compiler_params=_cparams(("parallel",)),
        name="rope_tables",
    )(pos_b, inv_tiled)


def _inproj_kernel(x_ref, w_ref, b_ref, wt_ref, bt_ref, cos_ref, sin_ref,
                   qn_ref, qs_ref, ks_ref, kw_ref, kswa_ref, kc_ref, vc_ref,
                   vst_ref, vwt_ref, vswat_ref, gt_ref):
    xb = x_ref[...].astype(BF16)
    cos = cos_ref[...]
    sin = sin_ref[...]

    def proj(lo, hi):
        return jnp.dot(xb, w_ref[:, lo:hi], preferred_element_type=F32) + b_ref[:, lo:hi]

    def roped(tile_lo, n_tiles, out_ref):
        lo, hi = tile_lo * LANES, (tile_lo + n_tiles) * LANES
        h = proj(lo, hi)
        hr = proj(ROPE_W + lo, ROPE_W + hi)
        for t in range(n_tiles):
            sl = slice(t * LANES, (t + 1) * LANES)
            out_ref[:, sl] = (h[:, sl] * cos + hr[:, sl] * sin).astype(out_ref.dtype)

    roped(0, 4, qn_ref)
    roped(4, 4, qs_ref)
    roped(8, 1, ks_ref)
    roped(9, 1, kw_ref)
    roped(10, 1, kswa_ref)
    plain = proj(2 * ROPE_W, PROJ_W)
    kc_ref[...] = plain[:, :LANES].astype(kc_ref.dtype)
    vc_ref[...] = plain[:, LANES:].astype(vc_ref.dtype)
    tr = lax.dot_general(wt_ref[...], xb, _NT, preferred_element_type=F32) + bt_ref[...]
    for t, ref in enumerate((vst_ref, vwt_ref, vswat_ref)):
        ref[...] = tr[t * KV_W:(t + 1) * KV_W].astype(ref.dtype)
    gt_ref[...] = tr[3 * KV_W:]


def input_projection(x2d, w_all, b_all, wt, bt, cos, sin):
    n = x2d.shape[0]
    tm = min(n, 512)
    row = lambda w: pl.BlockSpec((tm, w), lambda i: (i, 0))
    col = lambda h: pl.BlockSpec((h, tm), lambda i: (0, i))
    outs = ([jax.ShapeDtypeStruct((n, Q_W), BF16)] * 2 + [jax.ShapeDtypeStruct((n, KV_W), BF16)] * 5
            + [jax.ShapeDtypeStruct((KV_W, n), BF16)] * 3 + [jax.ShapeDtypeStruct((GATE_ROWS, n), F32)])
    return pl.pallas_call(
        _inproj_kernel,
        out_shape=tuple(outs),
        grid=(n // tm,),
        in_specs=[row(D_MODEL), _full((D_MODEL, PROJ_W)), _full((1, PROJ_W)), _full((PROJ_T_ROWS, D_MODEL)),
                  _full((PROJ_T_ROWS, 1)), row(LANES), row(LANES)],
        out_specs=tuple([row(Q_W)] * 2 + [row(KV_W)] * 5 + [col(KV_W)] * 3 + [col(GATE_ROWS)]),
        compiler_params=_cparams(("parallel",)),
        name="input_projection",
    )(x2d, w_all, b_all, wt, bt, cos, sin)


def _prep_inproj_weights(w_in, b_in):
    w = jnp.concatenate([w_in[:, _ROPE_COLS] * _ROPE_SCALE, w_in[:, _ROT_COLS] * _ROT_SCALE, w_in[:, _PLAIN_COLS]], axis=1)
    b = jnp.concatenate([b_in[_ROPE_COLS] * _ROPE_SCALE, b_in[_ROT_COLS] * _ROT_SCALE, b_in[_PLAIN_COLS]])
    pad = GATE_ROWS - N_GATES
    wt = jnp.concatenate([w_in[:, _T_COLS].T, jnp.zeros((pad, D_MODEL), F32)], axis=0)
    bt = jnp.concatenate([b_in[_T_COLS], jnp.zeros((pad,), F32)])
    return w.astype(BF16), b[None, :], wt.astype(BF16), bt[:, None]


def _gelu_tanh(x):
    return 0.5 * x * (1.0 + jnp.tanh(np.sqrt(2.0 / np.pi) * (x + 0.044715 * (x * x * x))))


def _compress_kernel(kc_ref, vc_ref, cos_ref, sin_ref,
                     kw1a_ref, kw1b_ref, kpe_ref, kw2_ref, kw2r_ref,
                     vw1a_ref, vw1b_ref, vpe_ref, vw2t_ref,
                     ko_ref, vot_ref):
    def hidden(x_ref, w1a_ref, w1b_ref, pe_ref):
        x = x_ref[...]
        ya = jnp.dot(x, w1a_ref[...], preferred_element_type=F32)
        yb = jnp.dot(x, w1b_ref[...], preferred_element_type=F32)
        rows = ya.shape[0]
        h = ya + pltpu.roll(yb, rows - 1, 0)
        pe = pe_ref[...]
        peb = (jnp.dot(pe[:, :CMP_STRIDE * KV_W], w1a_ref[...], preferred_element_type=F32)
               + jnp.dot(pe[:, CMP_STRIDE * KV_W:], w1b_ref[...], preferred_element_type=F32))
        return _gelu_tanh(h + peb[0:1, :]).astype(BF16)

    ak = hidden(kc_ref, kw1a_ref, kw1b_ref, kpe_ref)
    kc = jnp.dot(ak, kw2_ref[...], preferred_element_type=F32)
    kcr = jnp.dot(ak, kw2r_ref[...], preferred_element_type=F32)
    ko_ref[...] = (kc * cos_ref[...] + kcr * sin_ref[...]).astype(ko_ref.dtype)
    av = hidden(vc_ref, vw1a_ref, vw1b_ref, vpe_ref)
    vot_ref[...] = lax.dot_general(vw2t_ref[...], av, _NT, preferred_element_type=F32).astype(vot_ref.dtype)


def _prep_compress_weights(pe, w1, w2, is_key):
    eye = jnp.eye(KV_GROUPS, dtype=F32)
    w1r = w1.reshape(CMP_BLOCK, HEAD_DIM, CMP_HIDDEN)

    def half(wh):
        return jnp.einsum("idh,ge->igdeh", wh, eye).reshape(CMP_STRIDE * KV_W, KV_GROUPS * CMP_HIDDEN).astype(BF16)

    def block_diag(w):
        wb = jnp.einsum("hd,ge->ghed", w, eye)
        return wb.reshape(KV_GROUPS * CMP_HIDDEN, KV_W)

    w1a, w1b = half(w1r[:CMP_STRIDE]), half(w1r[CMP_STRIDE:])
    pe_row = jnp.broadcast_to(pe[:, None, :], (CMP_BLOCK, KV_GROUPS, HEAD_DIM)).reshape(1, CMP_BLOCK * KV_W)
    pe_rows = jnp.broadcast_to(pe_row, (8, CMP_BLOCK * KV_W)).astype(BF16)
    if not is_key:
        return [w1a, w1b, pe_rows, block_diag(w2).T.astype(BF16)]
    half_d = HEAD_DIM // 2
    w2rot = jnp.concatenate([-w2[:, half_d:], w2[:, :half_d]], axis=1)
    return [w1a, w1b, pe_rows, block_diag(w2).astype(BF16), block_diag(w2rot).astype(BF16)]


def compress_kv(kc_chunks, vc_chunks, cos_c, sin_c, kweights, vweights):
    rows = kc_chunks.shape[0]
    tm = min(rows, 512)
    cw = CMP_STRIDE * KV_W
    hw = KV_GROUPS * CMP_HIDDEN
    row = lambda w: pl.BlockSpec((tm, w), lambda i: (i, 0))
    wspecs_k = [_full((cw, hw)), _full((cw, hw)), _full((8, 2 * cw)), _full((hw, KV_W)), _full((hw, KV_W))]
    wspecs_v = wspecs_k[:3] + [_full((KV_W, hw))]
    return pl.pallas_call(
        _compress_kernel,
        out_shape=(jax.ShapeDtypeStruct((rows, KV_W), BF16), jax.ShapeDtypeStruct((KV_W, rows), BF16)),
        grid=(rows // tm,),
        in_specs=[row(cw), row(cw), row(LANES), row(LANES)] + wspecs_k + wspecs_v,
        out_specs=(row(KV_W), pl.BlockSpec((KV_W, tm), lambda i: (0, i))),
        compiler_params=_cparams(("parallel",)),
        name="compress_kv",
    )(kc_chunks, vc_chunks, cos_c, sin_c, *kweights, *vweights)


def _attn_kernel(sinks_ref, qn_ref, qs_ref, gt_ref, kc_ref, vct_ref,
                 ks_ref, vst_ref, kw_ref, vwt_ref, ksw_ref, vswt_ref, pool_ref, o_ref,
                 sel_ref, sc_slc, sc_win, sc_swa):
    i = pl.program_id(1)
    c = Q_CHUNK
    cols = Q_HEADS * c
    key = lax.broadcasted_iota(jnp.int32, (c, c), 0)
    tq = lax.broadcasted_iota(jnp.int32, (c, c), 1)
    t_minus_key = tq - key
    lane = lax.broadcasted_iota(jnp.int32, (c, LANES), 1)
    group_mask = [jnp.where(lane < HEAD_DIM, 1.0, 0.0).astype(BF16), jnp.where(lane < HEAD_DIM, 0.0, 1.0).astype(BF16)]

    def rep(a, times=Q_HEADS):
        return jnp.concatenate([a] * times, axis=1)

    def stack_q(q_ref):
        return jnp.concatenate([q_ref[:, r * LANES:(r + 1) * LANES] * group_mask[g]
                                for g in range(KV_GROUPS) for r in range(GROUP_HEADS)], axis=0)

    sub = c // 8

    def part_max(x):
        return jnp.max(x.reshape(sub, 8, x.shape[-1]), axis=0)

    def part_sum(x):
        return jnp.sum(x.reshape(sub, 8, x.shape[-1]), axis=0)

    def score_block(q, k_blk, mask, sc_ref, row0, mpart):
        s = lax.dot_general(k_blk, q, _NT, preferred_element_type=F32)
        s = jnp.where(mask, s, NEG)
        sc_ref[pl.ds(row0, c), :] = s
        return jnp.maximum(mpart, part_max(s))

    def prob_block(vt_blk, sc_ref, row0, m, carry):
        lpart, acc = carry
        p = jnp.exp2(sc_ref[pl.ds(row0, c), :] - m)
        return lpart + part_sum(p), acc + jnp.dot(vt_blk, p.astype(BF16), preferred_element_type=F32)

    mpart0 = jnp.full((8, cols), NEG, F32)
    carry0 = (jnp.zeros((8, cols), F32), jnp.zeros((KV_W, cols), F32))

    def banded(q, k_ref, vt_ref, window, sc_ref, sink=None):
        nblk = _band_blocks(window)
        offs = []
        mpart = mpart0
        for j in range(nblk):
            back = nblk - 1 - j
            kb = i - back
            off = pl.multiple_of(jnp.maximum(kb, 0) * c, c)
            offs.append(off)
            width = jnp.where(kb >= 0, window, 0).astype(jnp.uint32)
            mask = (t_minus_key + back * c).astype(jnp.uint32) < width
            mpart = score_block(q, k_ref[pl.ds(off, c), :], rep(mask), sc_ref, j * c, mpart)
        m = jnp.max(mpart, axis=0, keepdims=True)
        if sink is not None:
            m = jnp.maximum(m, sink)
        carry = carry0
        for j in range(nblk):
            carry = prob_block(vt_ref[:, pl.ds(offs[j], c)], sc_ref, j * c, m, carry)
        l = jnp.sum(carry[0], axis=0, keepdims=True)
        if sink is not None:
            l = l + jnp.exp2(sink - m)
        return carry[1] * (1.0 / jnp.maximum(l, 1e-30))

    def compressed(q):
        s = lax.dot_general(kc_ref[...], q, _NT, preferred_element_type=F32)
        cend = key * CMP_STRIDE + (CMP_BLOCK - 1)
        mk = rep(cend <= tq + i * c)
        s = jnp.where(mk, s, NEG)
        m = jnp.max(s, axis=0, keepdims=True)
        e = jnp.where(mk, jnp.exp2(s - m), 0.0)
        p = e * (1.0 / jnp.maximum(jnp.sum(e, axis=0, keepdims=True), 1e-30))
        o = jnp.dot(vct_ref[...], p.astype(BF16), preferred_element_type=F32)
        psums = []
        for g in range(KV_GROUPS):
            base = g * GROUP_HEADS * c
            psums.append(p[:, base:base + c] + p[:, base + c:base + 2 * c]
                         + p[:, base + 2 * c:base + 3 * c] + p[:, base + 3 * c:base + 4 * c])
        return o, psums

    nsel = pool_ref.shape[0]
    blk = lax.broadcasted_iota(jnp.int32, (nsel, c), 0)
    tcol = lax.broadcasted_iota(jnp.int32, (nsel, c), 1)

    def select_blocks(psum, g):
        hi = psum.astype(BF16)
        lo = (psum - hi.astype(F32)).astype(BF16)
        pool = pool_ref[...]
        imp = jnp.dot(pool, hi, preferred_element_type=F32) + jnp.dot(pool, lo, preferred_element_type=F32)
        t = tcol + i * c
        cur = t >> 6
        forced = (blk == 0) | (blk == cur) | (blk == cur - 1)
        valid = (blk << 6) <= t
        score = jnp.where(forced, FORCE, jnp.where(valid, imp, -1.0))
        rank = jnp.zeros((nsel, c), F32)
        for k in range(nsel):
            sk = score[k:k + 1, :]
            tie = jnp.where(blk > k, 1.0, 0.0)
            rank = rank + jnp.where(sk > score, 1.0, jnp.where(sk == score, tie, 0.0))
        sel_ref[g] = jnp.where(rank < SEL_TOPN, 1.0, 0.0)

    def selected(q, sc_ref):
        nquad = (i + SLC_UNROLL) // SLC_UNROLL

        def block_mask(kb):
            causal = t_minus_key + (i - kb) * c >= 0
            per_group = []
            for g in range(KV_GROUPS):
                first = sel_ref[g, pl.ds(2 * kb, 1), :]
                second = sel_ref[g, pl.ds(2 * kb + 1, 1), :]
                chosen = jnp.where(key < SEL_BLOCK, first, second) > 0.5
                per_group.append(rep(chosen & causal, GROUP_HEADS))
            return jnp.concatenate(per_group, axis=1)

        def phase_a(qd, mpart):
            for u in range(SLC_UNROLL):
                kb = qd * SLC_UNROLL + u
                off = pl.multiple_of(kb * c, c)
                mpart = score_block(q, ks_ref[pl.ds(off, c), :], block_mask(kb), sc_ref, off, mpart)
            return mpart

        m = jnp.max(lax.fori_loop(0, nquad, phase_a, mpart0), axis=0, keepdims=True)

        def phase_b(qd, carry):
            for u in range(SLC_UNROLL):
                off = pl.multiple_of((qd * SLC_UNROLL + u) * c, c)
                carry = prob_block(vst_ref[:, pl.ds(off, c)], sc_ref, off, m, carry)
            return carry

        lpart, acc = lax.fori_loop(0, nquad, phase_b, carry0)
        return acc * (1.0 / jnp.maximum(jnp.sum(lpart, axis=0, keepdims=True), 1e-30))

    def head_block(o, h):
        g = h // GROUP_HEADS
        return o[g * HEAD_DIM:(g + 1) * HEAD_DIM, h * c:(h + 1) * c]

    gs = jax.nn.sigmoid(gt_ref[...])
    q = stack_q(qn_ref)
    o_cmp, psums = compressed(q)
    for g in range(KV_GROUPS):
        select_blocks(psums[g], g)
    o_slc = selected(q, sc_slc)
    o_win = banded(q, kw_ref, vwt_ref, NSA_WINDOW, sc_win)
    heads = [gs[3 * h:3 * h + 1] * head_block(o_cmp, h) + gs[3 * h + 1:3 * h + 2] * head_block(o_slc, h)
             + gs[3 * h + 2:3 * h + 3] * head_block(o_win, h) for h in range(Q_HEADS)]

    head_col = lax.broadcasted_iota(jnp.int32, (1, cols), 1) // c
    sink = jnp.zeros((1, cols), F32)
    for h in range(Q_HEADS):
        sink = jnp.where(head_col == h, sinks_ref[h] * LOG2E, sink)
    o_swa = banded(stack_q(qs_ref), ksw_ref, vswt_ref, SWA_WINDOW, sc_swa, sink)
    heads += [head_block(o_swa, h) for h in range(Q_HEADS)]
    o_ref[...] = jnp.concatenate(heads, axis=0).T.astype(o_ref.dtype)


def _band_blocks(window):
    return (window - 1 + Q_CHUNK - 1) // Q_CHUNK + 1


def attention(sinks, qn, qs, gates_t, kc_c, vc_ct, ks, vst, kw, vwt, ksw, vswt, pool, batch, seq):
    n = batch * seq
    nq = seq // Q_CHUNK
    assert seq // SEL_BLOCK == pool.shape[0] and seq // CMP_STRIDE == LANES
    qspec = lambda w: pl.BlockSpec((Q_CHUNK, w), lambda b, i, *_: (b * nq + i, 0))
    kspec = pl.BlockSpec((seq, KV_W), lambda b, i, *_: (b, 0))
    vtspec = pl.BlockSpec((KV_W, seq), lambda b, i, *_: (0, b))
    grid_spec = pltpu.PrefetchScalarGridSpec(
        num_scalar_prefetch=1,
        grid=(batch, nq),
        in_specs=[qspec(Q_W), qspec(Q_W), pl.BlockSpec((GATE_ROWS, Q_CHUNK), lambda b, i, *_: (0, b * nq + i)),
                  pl.BlockSpec((LANES, KV_W), lambda b, i, *_: (b, 0)), pl.BlockSpec((KV_W, LANES), lambda b, i, *_: (0, b)),
                  kspec, vtspec, kspec, vtspec, kspec, vtspec, _full(pool.shape)],
        out_specs=qspec(2 * Q_W),
        scratch_shapes=[pltpu.VMEM((KV_GROUPS,) + pool.shape[:1] + (Q_CHUNK,), F32),
                        pltpu.VMEM((seq, Q_HEADS * Q_CHUNK), F32),
                        pltpu.VMEM((_band_blocks(NSA_WINDOW) * Q_CHUNK, Q_HEADS * Q_CHUNK), F32),
                        pltpu.VMEM((_band_blocks(SWA_WINDOW) * Q_CHUNK, Q_HEADS * Q_CHUNK), F32)],
    )
    return pl.pallas_call(
        _attn_kernel,
        out_shape=jax.ShapeDtypeStruct((n, 2 * Q_W), BF16),
        grid_spec=grid_spec,
        compiler_params=_cparams(("parallel", "arbitrary")),
        name="attention",
    )(sinks, qn, qs, gates_t, kc_c, vc_ct, ks, vst, kw, vwt, ksw, vswt, pool)


ROW_SLABS = D_MODEL // LANES


def _store_row_slabs(ref, val):
    rows = val.shape[0]
    for s in range(ROW_SLABS):
        ref[pl.ds(s, rows, stride=ROW_SLABS), :] = val[:, s * LANES:(s + 1) * LANES]


def _load_row_slabs(ref, first_row, rows):
    return jnp.concatenate([ref[pl.ds(first_row * ROW_SLABS + s, rows, stride=ROW_SLABS), :]
                            for s in range(ROW_SLABS)], axis=1)


def _layer_norm(y, g, b):
    mu = jnp.mean(y, axis=-1, keepdims=True)
    yc = y - mu
    var = jnp.mean(yc * yc, axis=-1, keepdims=True)
    return yc * lax.rsqrt(var + LN_EPS) * g + b


def _post_attn_kernel(o_ref, x_ref, p_ref, wo_ref, bo_ref, g1_ref, b1_ref,
                      wrh_ref, wrl_ref, br_ref, wpg_ref, bpg_ref, wpp_ref,
                      x1_ref, e_ref, route_ref):
    a = jnp.dot(o_ref[...], wo_ref[...], preferred_element_type=F32) + bo_ref[...]
    x1 = _layer_norm(DN_ALPHA * x_ref[...] + a, g1_ref[...], b1_ref[...])
    _store_row_slabs(x1_ref, x1)
    xh = x1.astype(BF16)
    gate = jax.nn.sigmoid(jnp.dot(xh, wpg_ref[...], preferred_element_type=F32) + bpg_ref[...])
    e_ref[...] = gate * jnp.dot(p_ref[...].astype(BF16), wpp_ref[...], preferred_element_type=F32)
    xl = (x1 - xh.astype(F32)).astype(BF16)
    logits = (lax.dot_general(wrh_ref[...], xh, _NT, preferred_element_type=F32)
              + lax.dot_general(wrh_ref[...], xl, _NT, preferred_element_type=F32)
              + lax.dot_general(wrl_ref[...], xh, _NT, preferred_element_type=F32)) + br_ref[...]
    eidx = lax.broadcasted_iota(jnp.int32, logits.shape, 0).astype(F32)
    vals, idxs = [], []
    for _ in range(TOP_K):
        v = jnp.max(logits, axis=0, keepdims=True)
        ix = jnp.min(jnp.where(logits == v, eidx, float(N_EXPERTS)), axis=0, keepdims=True)
        logits = jnp.where(eidx == ix, -jnp.inf, logits)
        vals.append(v)
        idxs.append(ix)
    exps = [jnp.exp(v - vals[0]) for v in vals]
    den = exps[0] + exps[1] + exps[2] + exps[3]
    route_ref[...] = jnp.concatenate(idxs + [ex / den for ex in exps], axis=0)


def post_attention(o, x2d, p2d, wo, bo, g1, b1, wrh, wrl, br, wpg, bpg, wpp):
    n = x2d.shape[0]
    tm = min(n, 256)
    row = lambda w: pl.BlockSpec((tm, w), lambda i: (i, 0))
    return pl.pallas_call(
        _post_attn_kernel,
        out_shape=(jax.ShapeDtypeStruct((n * ROW_SLABS, LANES), F32), jax.ShapeDtypeStruct((n, D_MODEL), F32),
                   jax.ShapeDtypeStruct((2 * TOP_K, n), F32)),
        grid=(n // tm,),
        in_specs=[row(D_MODEL), row(D_MODEL), row(P_DIM), _full((D_MODEL, D_MODEL)), _full((1, D_MODEL)),
                  _full((1, D_MODEL)), _full((1, D_MODEL)), _full((N_EXPERTS, D_MODEL)), _full((N_EXPERTS, D_MODEL)),
                  _full((N_EXPERTS, 1)), _full((D_MODEL, D_MODEL)), _full((1, D_MODEL)), _full((P_DIM, D_MODEL))],
        out_specs=(pl.BlockSpec((tm * ROW_SLABS, LANES), lambda i: (i, 0)), row(D_MODEL),
                   pl.BlockSpec((2 * TOP_K, tm), lambda i: (0, i))),
        compiler_params=_cparams(("parallel",)),
        name="post_attention",
    )(o, x2d, p2d, wo, bo, g1, b1, wrh, wrl, br, wpg, bpg, wpp)


MOE_TILE = 256
DMA_ISSUE_UNROLL = 8


def _route_tables(expert_idx, tm):
    k, n = expert_idx.shape
    a = k * n
    n_rows = a + (N_EXPERTS + 1) * tm
    flat_e = expert_idx.reshape(a)
    order = jnp.argsort(flat_e, stable=True).astype(jnp.int32)
    cnt = jnp.sum((flat_e[:, None] == jnp.arange(N_EXPERTS, dtype=jnp.int32)[None, :]).astype(jnp.int32), axis=0)
    start = jnp.cumsum(cnt) - cnt
    pcnt = ((cnt + tm - 1) // tm) * tm
    pend = jnp.cumsum(pcnt)
    pstart = pend - pcnt
    q = jnp.arange(n_rows, dtype=jnp.int32)
    e_of_q = jnp.sum((q[:, None] >= pend[None, :]).astype(jnp.int32), axis=1)
    e_c = jnp.minimum(e_of_q, N_EXPERTS - 1)
    local = q - pstart[e_c]
    valid = (e_of_q < N_EXPERTS) & (local < cnt[e_c])
    r = jnp.clip(start[e_c] + local, 0, a - 1)
    src_tok = jnp.where(valid, order[r] % n, 0).astype(jnp.int32)
    inv = jnp.argsort(order).astype(jnp.int32)
    dest = (pstart[flat_e] + inv - start[flat_e]).astype(jnp.int32)
    nt = n_rows // tm
    tile_e = e_c[::tm]
    tile_active = (jnp.arange(nt, dtype=jnp.int32) * tm < pend[-1]).astype(jnp.int32)
    prev = jnp.concatenate([jnp.full((1,), -1, jnp.int32), tile_e[:-1]])
    tile_first = ((tile_e != prev) & (tile_active > 0)).astype(jnp.int32)
    return tile_e, tile_active, tile_first, src_tok.reshape(nt, 1, tm), dest.reshape(k, n)


def _moe_kernel(te_ref, act_ref, first_ref, tok_ref, tok_next_ref,
                x_hbm, w1_ref, b1_ref, w2_ref, b2_ref, y_ref,
                xbuf, w1b, w2b, sem):
    i = pl.program_id(0)
    tm = xbuf.shape[1] // ROW_SLABS
    slot = i % 2

    def row_copy(tref, r, s):
        src = pl.multiple_of(tref[0, 0, r] * ROW_SLABS, ROW_SLABS)
        return pltpu.make_async_copy(x_hbm.at[pl.ds(src, ROW_SLABS)], xbuf.at[s, pl.ds(r * ROW_SLABS, ROW_SLABS)], sem.at[s])

    def wait_slot(s):
        pltpu.make_async_copy(x_hbm.at[pl.ds(0, tm * ROW_SLABS)], xbuf.at[s], sem.at[s]).wait()

    active = act_ref[i] > 0
    requested = jnp.where(i == 0, act_ref[0], act_ref[jnp.maximum(i - 1, 0)]) > 0

    @pl.when((i == 0) & active)
    def _():
        def body(r, carry):
            row_copy(tok_ref, r, 0).start()
            return carry
        lax.fori_loop(0, tm, body, 0, unroll=DMA_ISSUE_UNROLL)

    @pl.when(active)
    def _():
        wait_slot(slot)

        @pl.when(first_ref[i] > 0)
        def _():
            w1b[...] = w1_ref[0].astype(BF16)
            w2b[...] = w2_ref[0].astype(BF16)

        for r in range(tm):
            row_copy(tok_next_ref, r, 1 - slot).start()

        xb = _load_row_slabs(xbuf.at[slot], 0, tm).astype(BF16)
        h = jnp.dot(xb, w1b[...], preferred_element_type=F32) + b1_ref[0]
        gate = jnp.minimum(h[:, :D_FF], SWIGLU_LIMIT)
        up = jnp.clip(h[:, D_FF:], -SWIGLU_LIMIT, SWIGLU_LIMIT)
        act = (up + 1.0) * gate * jax.nn.sigmoid(SWIGLU_ALPHA * gate)
        _store_row_slabs(y_ref, jnp.dot(act.astype(BF16), w2b[...], preferred_element_type=F32) + b2_ref[0])

    @pl.when(jnp.logical_not(active))
    def _():
        y_ref[...] = jnp.zeros_like(y_ref)

    @pl.when(requested & jnp.logical_not(active))
    def _():
        wait_slot(slot)


def routed_experts(x1, tile_e, tile_active, tile_first, src_tok, w_e1, b_e1, w_e2, b_e2):
    nt, _, tm = src_tok.shape
    tokspec = lambda f: pl.BlockSpec((1, 1, tm), f, memory_space=pltpu.SMEM)
    grid_spec = pltpu.PrefetchScalarGridSpec(
        num_scalar_prefetch=3,
        grid=(nt,),
        in_specs=[tokspec(lambda i, *_: (i, 0, 0)),
                  tokspec(lambda i, *_: (jnp.minimum(i + 1, nt - 1), 0, 0)),
                  pl.BlockSpec(memory_space=pl.ANY),
                  pl.BlockSpec((1, D_MODEL, 2 * D_FF), lambda i, te, *_: (te[i], 0, 0)),
                  pl.BlockSpec((1, 1, 2 * D_FF), lambda i, te, *_: (te[i], 0, 0)),
                  pl.BlockSpec((1, D_FF, D_MODEL), lambda i, te, *_: (te[i], 0, 0)),
                  pl.BlockSpec((1, 1, D_MODEL), lambda i, te, *_: (te[i], 0, 0))],
        out_specs=pl.BlockSpec((tm * ROW_SLABS, LANES), lambda i, *_: (i, 0)),
        scratch_shapes=[pltpu.VMEM((2, tm * ROW_SLABS, LANES), F32), pltpu.VMEM((D_MODEL, 2 * D_FF), BF16),
                        pltpu.VMEM((D_FF, D_MODEL), BF16), pltpu.SemaphoreType.DMA((2,))],
    )
    return pl.pallas_call(
        _moe_kernel,
        out_shape=jax.ShapeDtypeStruct((nt * tm * ROW_SLABS, LANES), F32),
        grid_spec=grid_spec,
        compiler_params=_cparams(("arbitrary",)),
        name="routed_experts",
    )(tile_e, tile_active, tile_first, src_tok, src_tok, x1, w_e1, b_e1.reshape(N_EXPERTS, 1, 2 * D_FF),
      w_e2, b_e2.reshape(N_EXPERTS, 1, D_MODEL))


COMBINE_TILE = 128


def _combine_kernel(pos_ref, pos_next_ref, x1_ref, e_ref, gw_ref, g2_ref, b2_ref, ys_hbm, out_ref, buf, sem):
    i = pl.program_id(0)
    nt = pl.num_programs(0)
    tm = out_ref.shape[0]
    n_rows = TOP_K * tm
    slot = i % 2

    def row_copy(pref, r, s):
        src = pl.multiple_of(pref[0, 0, r] * ROW_SLABS, ROW_SLABS)
        return pltpu.make_async_copy(ys_hbm.at[pl.ds(src, ROW_SLABS)], buf.at[s, pl.ds(r * ROW_SLABS, ROW_SLABS)], sem.at[s])

    def wait_slot(s):
        pltpu.make_async_copy(ys_hbm.at[pl.ds(0, n_rows * ROW_SLABS)], buf.at[s], sem.at[s]).wait()

    @pl.when(i == 0)
    def _():
        def body(r, carry):
            row_copy(pos_ref, r, 0).start()
            return carry
        lax.fori_loop(0, n_rows, body, 0, unroll=DMA_ISSUE_UNROLL)

    wait_slot(slot)
    for r in range(n_rows):
        row_copy(pos_next_ref, r, 1 - slot).start()

    gw = gw_ref[...]
    m = jnp.zeros((tm, D_MODEL), F32)
    for k in range(TOP_K):
        m = m + gw[:, k:k + 1] * _load_row_slabs(buf.at[slot], k * tm, tm)
    out_ref[...] = _layer_norm(DN_ALPHA * _load_row_slabs(x1_ref, 0, tm) + m + e_ref[...], g2_ref[...], b2_ref[...])

    @pl.when(i == nt - 1)
    def _():
        wait_slot(1 - slot)


def combine_experts(pos, x1, e, gw, g2, b2, ys):
    n = e.shape[0]
    tm = min(n, COMBINE_TILE)
    nt = n // tm
    pos_t = pos.reshape(TOP_K, nt, tm).transpose(1, 0, 2).reshape(nt, 1, TOP_K * tm)
    posspec = lambda f: pl.BlockSpec((1, 1, TOP_K * tm), f, memory_space=pltpu.SMEM)
    row = lambda w: pl.BlockSpec((tm, w), lambda i: (i, 0))
    return pl.pallas_call(
        _combine_kernel,
        out_shape=jax.ShapeDtypeStruct((n, D_MODEL), F32),
        grid=(nt,),
        in_specs=[posspec(lambda i: (i, 0, 0)), posspec(lambda i: (jnp.minimum(i + 1, nt - 1), 0, 0)),
                  pl.BlockSpec((tm * ROW_SLABS, LANES), lambda i: (i, 0)), row(D_MODEL), row(TOP_K),
                  _full((1, D_MODEL)), _full((1, D_MODEL)), pl.BlockSpec(memory_space=pl.ANY)],
        out_specs=row(D_MODEL),
        scratch_shapes=[pltpu.VMEM((2, TOP_K * tm * ROW_SLABS, LANES), F32), pltpu.SemaphoreType.DMA((2,))],
        compiler_params=_cparams(("arbitrary",)),
        name="combine_experts",
    )(pos_t, pos_t, x1, e, gw, g2, b2, ys)


def kernel(x, p, positions, w_in, b_in, ck_pe, w_ck1, w_ck2, cv_pe, w_cv1, w_cv2, sinks, w_o, b_o,
           ln1_g, ln1_b, w_r, b_r, w_e1, b_e1, w_e2, b_e2, w_pg, b_pg, w_pp, ln2_g, ln2_b):
    batch, seq, _ = x.shape
    n = batch * seq
    n_cmp = seq // CMP_STRIDE
    inv = 1.0 / (ROPE_THETA ** (jnp.arange(0, HEAD_DIM, 2, dtype=F32) / HEAD_DIM))
    inv_tiled = jnp.tile(inv, LANES // (HEAD_DIM // 2))[None, :]
    cos, sin = rope_tables(positions.reshape(n), inv_tiled)
    pos_c = jnp.concatenate([positions[:, CMP_BLOCK - 1::CMP_STRIDE], positions[:, -1:]], axis=1)
    cos_c, sin_c = rope_tables(pos_c.reshape(batch * n_cmp), inv_tiled)
    pool = jnp.asarray(_pool_matrix(), BF16)

    x2d = x.reshape(n, D_MODEL)
    for li in range(DEPTH):
        qn, qs, ks, kw, kswa, kc, vc, vst, vwt, vswat, gates_t = input_projection(
            x2d, *_prep_inproj_weights(w_in[li], b_in[li]), cos, sin)
        kweights = _prep_compress_weights(ck_pe[li], w_ck1[li], w_ck2[li], True)
        vweights = _prep_compress_weights(cv_pe[li], w_cv1[li], w_cv2[li], False)
        kc_c, vc_ct = compress_kv(kc.reshape(batch * n_cmp, CMP_STRIDE * KV_W), vc.reshape(batch * n_cmp, CMP_STRIDE * KV_W),
                                  cos_c, sin_c, kweights, vweights)
        o = attention(sinks[li], qn, qs, gates_t, kc_c, vc_ct, ks, vst, kw, vwt, kswa, vswat, pool, batch, seq)

        wr_t = w_r[li].T
        wrh = wr_t.astype(BF16)
        wrl = (wr_t - wrh.astype(F32)).astype(BF16)
        x1, e, route = post_attention(o, x2d, p[li].reshape(n, P_DIM), w_o[li].astype(BF16), b_o[li][None, :],
                                      ln1_g[li][None, :], ln1_b[li][None, :], wrh, wrl, b_r[li][:, None],
                                      w_pg[li].astype(BF16), b_pg[li][None, :], w_pp[li].astype(BF16))
        expert_idx = route[:TOP_K].astype(jnp.int32)
        gw = route[TOP_K:].T
        tile_e, tile_active, tile_first, src_tok, pos = _route_tables(expert_idx, MOE_TILE)
        ys = routed_experts(x1, tile_e, tile_active, tile_first, src_tok, w_e1[li], b_e1[li], w_e2[li], b_e2[li])
        x2d = combine_experts(pos, x1, e, gw, ln2_g[li][None, :], ln2_b[li][None, :], ys)
    return x2d.reshape(batch, seq, D_MODEL)
```

```python
import numpy as np
import jax
import jax.numpy as jnp
from jax import lax
from jax.experimental import pallas as pl
from jax.experimental.pallas import tpu as pltpu

F32 = jnp.float32
BF16 = jnp.bfloat16

D_MODEL = 1024
DEPTH = 2
HEAD_DIM = 64
Q_HEADS = 8
KV_GROUPS = 2
GROUP_HEADS = Q_HEADS // KV_GROUPS
ROPE_THETA = 10000.0
CMP_STRIDE = 16
CMP_BLOCK = 32
CMP_HIDDEN = 256
SEL_BLOCK = 64
SEL_TOPN = 8
NSA_WINDOW = 512
SWA_WINDOW = 128
N_EXPERTS = 32
TOP_K = 4
D_FF = D_MODEL
SWIGLU_LIMIT = 7.0
SWIGLU_ALPHA = 1.702
P_DIM = 256
DN_ALPHA = (2 * DEPTH) ** 0.25
LN_EPS = 1e-5
NEG = -1e30
FORCE = 1e9
LOG2E = float(np.log2(np.e))

LANES = 128
Q_CHUNK = 128
SLC_UNROLL = 4
KV_W = KV_GROUPS * HEAD_DIM
Q_W = Q_HEADS * HEAD_DIM
VMEM_LIMIT = 56 * 1024 * 1024

_OFF_QN, _OFF_KC, _OFF_VC, _OFF_KS, _OFF_VS, _OFF_KW, _OFF_VW = 0, 512, 640, 768, 896, 1024, 1152
_OFF_GATES, _OFF_QS, _OFF_KSWA, _OFF_VSWA = 1280, 1304, 1816, 1944
N_GATES = Q_HEADS * 3
GATE_ROWS = 32

N_ROPE_TILES = 11
ROPE_W = N_ROPE_TILES * LANES
PROJ_W = 2 * ROPE_W + 2 * LANES
PROJ_T_ROWS = 3 * KV_W + GATE_ROWS


def _paired_head_cols(base):
    cols = []
    for j in range(GROUP_HEADS):
        for half in range(KV_GROUPS):
            h = j + GROUP_HEADS * half
            cols.extend(base + h * HEAD_DIM + d for d in range(HEAD_DIM))
    return cols


def _proj_layout():
    rope_cols = (_paired_head_cols(_OFF_QN) + _paired_head_cols(_OFF_QS)
                 + list(range(_OFF_KS, _OFF_KS + KV_W)) + list(range(_OFF_KW, _OFF_KW + KV_W))
                 + list(range(_OFF_KSWA, _OFF_KSWA + KV_W)))
    rope_cols = np.asarray(rope_cols, np.int32)
    scale = np.ones(ROPE_W, np.float64)
    scale[:2 * Q_W] = HEAD_DIM ** -0.5 * LOG2E
    pos_in_tile = np.arange(ROPE_W) % HEAD_DIM
    first_half = pos_in_tile < HEAD_DIM // 2
    rot_cols = np.where(first_half, rope_cols + HEAD_DIM // 2, rope_cols - HEAD_DIM // 2).astype(np.int32)
    rot_sign = np.where(first_half, -1.0, 1.0)
    plain_cols = np.concatenate([np.arange(o, o + KV_W) for o in (_OFF_KC, _OFF_VC)]).astype(np.int32)
    t_cols = np.concatenate([np.arange(o, o + KV_W) for o in (_OFF_VS, _OFF_VW, _OFF_VSWA)]
                            + [np.arange(_OFF_GATES, _OFF_GATES + N_GATES)]).astype(np.int32)
    return rope_cols, scale.astype(np.float32), rot_cols, (rot_sign * scale).astype(np.float32), plain_cols, t_cols


_ROPE_COLS, _ROPE_SCALE, _ROT_COLS, _ROT_SCALE, _PLAIN_COLS, _T_COLS = _proj_layout()


def _pool_matrix():
    m = np.zeros((32, LANES), np.float32)
    for c in range(LANES - 1):
        m[c // (SEL_BLOCK // CMP_STRIDE), c] = 1.0
    return m


def _cparams(sem, vmem=VMEM_LIMIT):
    return pltpu.CompilerParams(dimension_semantics=sem, vmem_limit_bytes=vmem)


def _full(shape):
    return pl.BlockSpec(shape, lambda *_: (0,) * len(shape))


_NT = (((1,), (1,)), ((), ()))


def _rope_table_kernel(pos_ref, inv_ref, cos_ref, sin_ref):
    ang = pos_ref[...] * inv_ref[...]
    cos_ref[...] = jnp.cos(ang)
    sin_ref[...] = jnp.sin(ang)


def rope_tables(pos, inv_tiled):
    m = pos.shape[0]
    pos_b = jnp.broadcast_to(pos.astype(F32)[:, None], (m, LANES))
    tm = min(m, 1024)
    spec = pl.BlockSpec((tm, LANES), lambda i: (i, 0))
    return pl.pallas_call(
        _rope_table_kernel,
        out_shape=(jax.ShapeDtypeStruct((m, LANES), F32),) * 2,
        grid=(m // tm,),
        in_specs=[spec, _full((1, LANES))],
        out_specs=(spec, spec),
        compiler_params=_cparams(("parallel",)),
        name="rope_tables",
    )(pos_b, inv_tiled)


def _inproj_kernel(x_ref, w_ref, b_ref, wt_ref, bt_ref, cos_ref, sin_ref,
                   qn_ref, qs_ref, ks_ref, kw_ref, kswa_ref, kc_ref, vc_ref,
                   vst_ref, vwt_ref, vswat_ref, gt_ref):
    xb = x_ref[...].astype(BF16)
    cos = cos_ref[...]
    sin = sin_ref[...]

    def proj(lo, hi):
        return jnp.dot(xb, w_ref[:, lo:hi], preferred_element_type=F32) + b_ref[:, lo:hi]

    def roped(tile_lo, n_tiles, out_ref):
        lo, hi = tile_lo * LANES, (tile_lo + n_tiles) * LANES
        h = proj(lo, hi)
        hr = proj(ROPE_W + lo, ROPE_W + hi)
        for t in range(n_tiles):
            sl = slice(t * LANES, (t + 1) * LANES)
            out_ref[:, sl] = (h[:, sl] * cos + hr[:, sl] * sin).astype(out_ref.dtype)

    roped(0, 4, qn_ref)
    roped(4, 4, qs_ref)
    roped(8, 1, ks_ref)
    roped(9, 1, kw_ref)
    roped(10, 1, kswa_ref)
    plain = proj(2 * ROPE_W, PROJ_W)
    kc_ref[...] = plain[:, :LANES].astype(kc_ref.dtype)
    vc_ref[...] = plain[:, LANES:].astype(vc_ref.dtype)
    tr = lax.dot_general(wt_ref[...], xb, _NT, preferred_element_type=F32) + bt_ref[...]
    for t, ref in enumerate((vst_ref, vwt_ref, vswat_ref)):
        ref[...] = tr[t * KV_W:(t + 1) * KV_W].astype(ref.dtype)
    gt_ref[...] = tr[3 * KV_W:]


def input_projection(x2d, w_all, b_all, wt, bt, cos, sin):
    n = x2d.shape[0]
    tm = min(n, 512)
    row = lambda w: pl.BlockSpec((tm, w), lambda i: (i, 0))
    col = lambda h: pl.BlockSpec((h, tm), lambda i: (0, i))
    outs = ([jax.ShapeDtypeStruct((n, Q_W), BF16)] * 2 + [jax.ShapeDtypeStruct((n, KV_W), BF16)] * 5
            + [jax.ShapeDtypeStruct((KV_W, n), BF16)] * 3 + [jax.ShapeDtypeStruct((GATE_ROWS, n), F32)])
    return pl.pallas_call(
        _inproj_kernel,
        out_shape=tuple(outs),
        grid=(n // tm,),
        in_specs=[row(D_MODEL), _full((D_MODEL, PROJ_W)), _full((1, PROJ_W)), _full((PROJ_T_ROWS, D_MODEL)),
                  _full((PROJ_T_ROWS, 1)), row(LANES), row(LANES)],
        out_specs=tuple([row(Q_W)] * 2 + [row(KV_W)] * 5 + [col(KV_W)] * 3 + [col(GATE_ROWS)]),
        compiler_params=_cparams(("parallel",)),
        name="input_projection",
    )(x2d, w_all, b_all, wt, bt, cos, sin)


def _prep_inproj_weights(w_in, b_in):
    w = jnp.concatenate([w_in[:, _ROPE_COLS] * _ROPE_SCALE, w_in[:, _ROT_COLS] * _ROT_SCALE, w_in[:, _PLAIN_COLS]], axis=1)
    b = jnp.concatenate([b_in[_ROPE_COLS] * _ROPE_SCALE, b_in[_ROT_COLS] * _ROT_SCALE, b_in[_PLAIN_COLS]])
    pad = GATE_ROWS - N_GATES
    wt = jnp.concatenate([w_in[:, _T_COLS].T, jnp.zeros((pad, D_MODEL), F32)], axis=0)
    bt = jnp.concatenate([b_in[_T_COLS], jnp.zeros((pad,), F32)])
    return w.astype(BF16), b[None, :], wt.astype(BF16), bt[:, None]


def _gelu_tanh(x):
    return 0.5 * x * (1.0 + jnp.tanh(np.sqrt(2.0 / np.pi) * (x + 0.044715 * (x * x * x))))


def _compress_kernel(kc_ref, vc_ref, cos_ref, sin_ref,
                     kw1a_ref, kw1b_ref, kpe_ref, kw2_ref, kw2r_ref,
                     vw1a_ref, vw1b_ref, vpe_ref, vw2t_ref,
                     ko_ref, vot_ref):
    def hidden(x_ref, w1a_ref, w1b_ref, pe_ref):
        x = x_ref[...]
        ya = jnp.dot(x, w1a_ref[...], preferred_element_type=F32)
        yb = jnp.dot(x, w1b_ref[...], preferred_element_type=F32)
        rows = ya.shape[0]
        h = ya + pltpu.roll(yb, rows - 1, 0)
        pe = pe_ref[...]
        peb = (jnp.dot(pe[:, :CMP_STRIDE * KV_W], w1a_ref[...], preferred_element_type=F32)
               + jnp.dot(pe[:, CMP_STRIDE * KV_W:], w1b_ref[...], preferred_element_type=F32))
        return _gelu_tanh(h + peb[0:1, :]).astype(BF16)

    ak = hidden(kc_ref, kw1a_ref, kw1b_ref, kpe_ref)
    kc = jnp.dot(ak, kw2_ref[...], preferred_element_type=F32)
    kcr = jnp.dot(ak, kw2r_ref[...], preferred_element_type=F32)
    ko_ref[...] = (kc * cos_ref[...] + kcr * sin_ref[...]).astype(ko_ref.dtype)
    av = hidden(vc_ref, vw1a_ref, vw1b_ref, vpe_ref)
    vot_ref[...] = lax.dot_general(vw2t_ref[...], av, _NT, preferred_element_type=F32).astype(vot_ref.dtype)


def _prep_compress_weights(pe, w1, w2, is_key):
    eye = jnp.eye(KV_GROUPS, dtype=F32)
    w1r = w1.reshape(CMP_BLOCK, HEAD_DIM, CMP_HIDDEN)

    def half(wh):
        return jnp.einsum("idh,ge->igdeh", wh, eye).reshape(CMP_STRIDE * KV_W, KV_GROUPS * CMP_HIDDEN).astype(BF16)

    def block_diag(w):
        wb = jnp.einsum("hd,ge->ghed", w, eye)
        return wb.reshape(KV_GROUPS * CMP_HIDDEN, KV_W)

    w1a, w1b = half(w1r[:CMP_STRIDE]), half(w1r[CMP_STRIDE:])
    pe_row = jnp.broadcast_to(pe[:, None, :], (CMP_BLOCK, KV_GROUPS, HEAD_DIM)).reshape(1, CMP_BLOCK * KV_W)
    pe_rows = jnp.broadcast_to(pe_row, (8, CMP_BLOCK * KV_W)).astype(BF16)
    if not is_key:
        return [w1a, w1b, pe_rows, block_diag(w2).T.astype(BF16)]
    half_d = HEAD_DIM // 2
    w2rot = jnp.concatenate([-w2[:, half_d:], w2[:, :half_d]], axis=1)
    return [w1a, w1b, pe_rows, block_diag(w2).astype(BF16), block_diag(w2rot).astype(BF16)]


def compress_kv(kc_chunks, vc_chunks, cos_c, sin_c, kweights, vweights):
    rows = kc_chunks.shape[0]
    tm = min(rows, 512)
    cw = CMP_STRIDE * KV_W
    hw = KV_GROUPS * CMP_HIDDEN
    row = lambda w: pl.BlockSpec((tm, w), lambda i: (i, 0))
    wspecs_k = [_full((cw, hw)), _full((cw, hw)), _full((8, 2 * cw)), _full((hw, KV_W)), _full((hw, KV_W))]
    wspecs_v = wspecs_k[:3] + [_full((KV_W, hw))]
    return pl.pallas_call(
        _compress_kernel,
        out_shape=(jax.ShapeDtypeStruct((rows, KV_W), BF16), jax.ShapeDtypeStruct((KV_W, rows), BF16)),
        grid=(rows // tm,),
        in_specs=[row(cw), row(cw), row(LANES), row(LANES)] + wspecs_k + wspecs_v,
        out_specs=(row(KV_W), pl.BlockSpec((KV_W, tm), lambda i: (0, i))),
        compiler_params=_cparams(("parallel",)),
        name="compress_kv",
    )(kc_chunks, vc_chunks, cos_c, sin_c, *kweights, *vweights)


def _attn_kernel(sinks_ref, qn_ref, qs_ref, gt_ref, kc_ref, vct_ref,
                 ks_ref, vst_ref, kw_ref, vwt_ref, ksw_ref, vswt_ref, pool_ref, o_ref,
                 sel_ref, sc_slc, sc_win, sc_swa):
    i = pl.program_id(1)
    c = Q_CHUNK
    cols = Q_HEADS * c
    key = lax.broadcasted_iota(jnp.int32, (c, c), 0)
    tq = lax.broadcasted_iota(jnp.int32, (c, c), 1)
    t_minus_key = tq - key
    lane = lax.broadcasted_iota(jnp.int32, (c, LANES), 1)
    group_mask = [jnp.where(lane < HEAD_DIM, 1.0, 0.0).astype(BF16), jnp.where(lane < HEAD_DIM, 0.0, 1.0).astype(BF16)]

    def rep(a, times=Q_HEADS):
        return jnp.concatenate([a] * times, axis=1)

    def stack_q(q_ref):
        return jnp.concatenate([q_ref[:, r * LANES:(r + 1) * LANES] * group_mask[g]
                                for g in range(KV_GROUPS) for r in range(GROUP_HEADS)], axis=0)

    sub = c // 8

    def part_max(x):
        return jnp.max(x.reshape(sub, 8, x.shape[-1]), axis=0)

    def part_sum(x):
        return jnp.sum(x.reshape(sub, 8, x.shape[-1]), axis=0)

    def score_block(q, k_blk, mask, sc_ref, row0, mpart):
        s = lax.dot_general(k_blk, q, _NT, preferred_element_type=F32)
        s = jnp.where(mask, s, NEG)
        sc_ref[pl.ds(row0, c), :] = s
        return jnp.maximum(mpart, part_max(s))

    def prob_block(vt_blk, sc_ref, row0, m, carry):
        lpart, acc = carry
        p = jnp.exp2(sc_ref[pl.ds(row0, c), :] - m)
        return lpart + part_sum(p), acc + jnp.dot(vt_blk, p.astype(BF16), preferred_element_type=F32)

    mpart0 = jnp.full((8, cols), NEG, F32)
    carry0 = (jnp.zeros((8, cols), F32), jnp.zeros((KV_W, cols), F32))

    def banded(q, k_ref, vt_ref, window, sc_ref, sink=None):
        nblk = _band_blocks(window)
        offs = []
        mpart = mpart0
        for j in range(nblk):
            back = nblk - 1 - j
            kb = i - back
            off = pl.multiple_of(jnp.maximum(kb, 0) * c, c)
            offs.append(off)
            width = jnp.where(kb >= 0, window, 0).astype(jnp.uint32)
            mask = (t_minus_key + back * c).astype(jnp.uint32) < width
            mpart = score_block(q, k_ref[pl.ds(off, c), :], rep(mask), sc_ref, j * c, mpart)
        m = jnp.max(mpart, axis=0, keepdims=True)
        if sink is not None:
            m = jnp.maximum(m, sink)
        carry = carry0
        for j in range(nblk):
            carry = prob_block(vt_ref[:, pl.ds(offs[j], c)], sc_ref, j * c, m, carry)
        l = jnp.sum(carry[0], axis=0, keepdims=True)
        if sink is not None:
            l = l + jnp.exp2(sink - m)
        return carry[1] * (1.0 / jnp.maximum(l, 1e-30))

    def compressed(q):
        s = lax.dot_general(kc_ref[...], q, _NT, preferred_element_type=F32)
        cend = key * CMP_STRIDE + (CMP_BLOCK - 1)
        mk = rep(cend <= tq + i * c)
        s = jnp.where(mk, s, NEG)
        m = jnp.max(s, axis=0, keepdims=True)
        e = jnp.where(mk, jnp.exp2(s - m), 0.0)
        p = e * (1.0 / jnp.maximum(jnp.sum(e, axis=0, keepdims=True), 1e-30))
        o = jnp.dot(vct_ref[...], p.astype(BF16), preferred_element_type=F32)
        psums = []
        for g in range(KV_GROUPS):
            base = g * GROUP_HEADS * c
            psums.append(p[:, base:base + c] + p[:, base + c:base + 2 * c]
                         + p[:, base + 2 * c:base + 3 * c] + p[:, base + 3 * c:base + 4 * c])
        return o, psums

    nsel = pool_ref.shape[0]
    blk = lax.broadcasted_iota(jnp.int32, (nsel, c), 0)
    tcol = lax.broadcasted_iota(jnp.int32, (nsel, c), 1)

    def select_blocks(psum, g):
        hi = psum.astype(BF16)
        lo = (psum - hi.astype(F32)).astype(BF16)
        pool = pool_ref[...]
        imp = jnp.dot(pool, hi, preferred_element_type=F32) + jnp.dot(pool, lo, preferred_element_type=F32)
        t = tcol + i * c
        cur = t >> 6
        forced = (blk == 0) | (blk == cur) | (blk == cur - 1)
        valid = (blk << 6) <= t
        score = jnp.where(forced, FORCE, jnp.where(valid, imp, -1.0))
        rank = jnp.zeros((nsel, c), F32)
        for k in range(nsel):
            sk = score[k:k + 1, :]
            tie = jnp.where(blk > k, 1.0, 0.0)
            rank = rank + jnp.where(sk > score, 1.0, jnp.where(sk == score, tie, 0.0))
        sel_ref[g] = jnp.where(rank < SEL_TOPN, 1.0, 0.0)

    def selected(q, sc_ref):
        nquad = (i + SLC_UNROLL) // SLC_UNROLL

        def block_mask(kb):
            causal = t_minus_key + (i - kb) * c >= 0
            per_group = []
            for g in range(KV_GROUPS):
                first = sel_ref[g, pl.ds(2 * kb, 1), :]
                second = sel_ref[g, pl.ds(2 * kb + 1, 1), :]
                chosen = jnp.where(key < SEL_BLOCK, first, second) > 0.5
                per_group.append(rep(chosen & causal, GROUP_HEADS))
            return jnp.concatenate(per_group, axis=1)

        def phase_a(qd, mpart):
            for u in range(SLC_UNROLL):
                kb = qd * SLC_UNROLL + u
                off = pl.multiple_of(kb * c, c)
                mpart = score_block(q, ks_ref[pl.ds(off, c), :], block_mask(kb), sc_ref, off, mpart)
            return mpart

        m = jnp.max(lax.fori_loop(0, nquad, phase_a, mpart0), axis=0, keepdims=True)

        def phase_b(qd, carry):
            for u in range(SLC_UNROLL):
                off = pl.multiple_of((qd * SLC_UNROLL + u) * c, c)
                carry = prob_block(vst_ref[:, pl.ds(off, c)], sc_ref, off, m, carry)
            return carry

        lpart, acc = lax.fori_loop(0, nquad, phase_b, carry0)
        return acc * (1.0 / jnp.maximum(jnp.sum(lpart, axis=0, keepdims=True), 1e-30))

    def head_block(o, h):
        g = h // GROUP_HEADS
        return o[g * HEAD_DIM:(g + 1) * HEAD_DIM, h * c:(h + 1) * c]

    gs = jax.nn.sigmoid(gt_ref[...])
    q = stack_q(qn_ref)
    o_cmp, psums = compressed(q)
    for g in range(KV_GROUPS):
        select_blocks(psums[g], g)
    o_slc = selected(q, sc_slc)
    o_win = banded(q, kw_ref, vwt_ref, NSA_WINDOW, sc_win)
    heads = [gs[3 * h:3 * h + 1] * head_block(o_cmp, h) + gs[3 * h + 1:3 * h + 2] * head_block(o_slc, h)
             + gs[3 * h + 2:3 * h + 3] * head_block(o_win, h) for h in range(Q_HEADS)]

    head_col = lax.broadcasted_iota(jnp.int32, (1, cols), 1) // c
    sink = jnp.zeros((1, cols), F32)
    for h in range(Q_HEADS):
        sink = jnp.where(head_col == h, sinks_ref[h] * LOG2E, sink)
    o_swa = banded(stack_q(qs_ref), ksw_ref, vswt_ref, SWA_WINDOW, sc_swa, sink)
    heads += [head_block(o_swa, h) for h in range(Q_HEADS)]
    o_ref[...] = jnp.concatenate(heads, axis=0).T.astype(o_ref.dtype)


def _band_blocks(window):
    return (window - 1 + Q_CHUNK - 1) // Q_CHUNK + 1


def attention(sinks, qn, qs, gates_t, kc_c, vc_ct, ks, vst, kw, vwt, ksw, vswt, pool, batch, seq):
    n = batch * seq
    nq = seq // Q_CHUNK
    assert seq // SEL_BLOCK == pool.shape[0] and seq // CMP_STRIDE == LANES
    qspec = lambda w: pl.BlockSpec((Q_CHUNK, w), lambda b, i, *_: (b * nq + i, 0))
    kspec = pl.BlockSpec((seq, KV_W), lambda b, i, *_: (b, 0))
    vtspec = pl.BlockSpec((KV_W, seq), lambda b, i, *_: (0, b))
    grid_spec = pltpu.PrefetchScalarGridSpec(
        num_scalar_prefetch=1,
        grid=(batch, nq),
        in_specs=[qspec(Q_W), qspec(Q_W), pl.BlockSpec((GATE_ROWS, Q_CHUNK), lambda b, i, *_: (0, b * nq + i)),
                  pl.BlockSpec((LANES, KV_W), lambda b, i, *_: (b, 0)), pl.BlockSpec((KV_W, LANES), lambda b, i, *_: (0, b)),
                  kspec, vtspec, kspec, vtspec, kspec, vtspec, _full(pool.shape)],
        out_specs=qspec(2 * Q_W),
        scratch_shapes=[pltpu.VMEM((KV_GROUPS,) + pool.shape[:1] + (Q_CHUNK,), F32),
                        pltpu.VMEM((seq, Q_HEADS * Q_CHUNK), F32),
                        pltpu.VMEM((_band_blocks(NSA_WINDOW) * Q_CHUNK, Q_HEADS * Q_CHUNK), F32),
                        pltpu.VMEM((_band_blocks(SWA_WINDOW) * Q_CHUNK, Q_HEADS * Q_CHUNK), F32)],
    )
    return pl.pallas_call(
        _attn_kernel,
        out_shape=jax.ShapeDtypeStruct((n, 2 * Q_W), BF16),
        grid_spec=grid_spec,
        compiler_params=_cparams(("parallel", "arbitrary")),
        name="attention",
    )(sinks, qn, qs, gates_t, kc_c, vc_ct, ks, vst, kw, vwt, ksw, vswt, pool)


ROW_SLABS = D_MODEL // LANES


def _store_row_slabs(ref, val):
    rows = val.shape[0]
    for s in range(ROW_SLABS):
        ref[pl.ds(s, rows, stride=ROW_SLABS), :] = val[:, s * LANES:(s + 1) * LANES]


def _load_row_slabs(ref, first_row, rows):
    return jnp.concatenate([ref[pl.ds(first_row * ROW_SLABS + s, rows, stride=ROW_SLABS), :]
                            for s in range(ROW_SLABS)], axis=1)


def _layer_norm(y, g, b):
    mu = jnp.mean(y, axis=-1, keepdims=True)
    yc = y - mu
    var = jnp.mean(yc * yc, axis=-1, keepdims=True)
    return yc * lax.rsqrt(var + LN_EPS) * g + b


def _post_attn_kernel(o_ref, x_ref, p_ref, wo_ref, bo_ref, g1_ref, b1_ref,
                      wrh_ref, wrl_ref, br_ref, wpg_ref, bpg_ref, wpp_ref,
                      x1_ref, e_ref, route_ref):
    a = jnp.dot(o_ref[...], wo_ref[...], preferred_element_type=F32) + bo_ref[...]
    x1 = _layer_norm(DN_ALPHA * x_ref[...] + a, g1_ref[...], b1_ref[...])
    _store_row_slabs(x1_ref, x1)
    xh = x1.astype(BF16)
    gate = jax.nn.sigmoid(jnp.dot(xh, wpg_ref[...], preferred_element_type=F32) + bpg_ref[...])
    e_ref[...] = gate * jnp.dot(p_ref[...].astype(BF16), wpp_ref[...], preferred_element_type=F32)
    xl = (x1 - xh.astype(F32)).astype(BF16)
    logits = (lax.dot_general(wrh_ref[...], xh, _NT, preferred_element_type=F32)
              + lax.dot_general(wrh_ref[...], xl, _NT, preferred_element_type=F32)
              + lax.dot_general(wrl_ref[...], xh, _NT, preferred_element_type=F32)) + br_ref[...]
    eidx = lax.broadcasted_iota(jnp.int32, logits.shape, 0).astype(F32)
    vals, idxs = [], []
    for _ in range(TOP_K):
        v = jnp.max(logits, axis=0, keepdims=True)
        ix = jnp.min(jnp.where(logits == v, eidx, float(N_EXPERTS)), axis=0, keepdims=True)
        logits = jnp.where(eidx == ix, -jnp.inf, logits)
        vals.append(v)
        idxs.append(ix)
    exps = [jnp.exp(v - vals[0]) for v in vals]
    den = exps[0] + exps[1] + exps[2] + exps[3]
    route_ref[...] = jnp.concatenate(idxs + [ex / den for ex in exps], axis=0)


def post_attention(o, x2d, p2d, wo, bo, g1, b1, wrh, wrl, br, wpg, bpg, wpp):
    n = x2d.shape[0]
    tm = min(n, 512)
    row = lambda w: pl.BlockSpec((tm, w), lambda i: (i, 0))
    return pl.pallas_call(
        _post_attn_kernel,
        out_shape=(jax.ShapeDtypeStruct((n * ROW_SLABS, LANES), F32), jax.ShapeDtypeStruct((n, D_MODEL), F32),
                   jax.ShapeDtypeStruct((2 * TOP_K, n), F32)),
        grid=(n // tm,),
        in_specs=[row(D_MODEL), row(D_MODEL), row(P_DIM), _full((D_MODEL, D_MODEL)), _full((1, D_MODEL)),
                  _full((1, D_MODEL)), _full((1, D_MODEL)), _full((N_EXPERTS, D_MODEL)), _full((N_EXPERTS, D_MODEL)),
                  _full((N_EXPERTS, 1)), _full((D_MODEL, D_MODEL)), _full((1, D_MODEL)), _full((P_DIM, D_MODEL))],
        out_specs=(pl.BlockSpec((tm * ROW_SLABS, LANES), lambda i: (i, 0)), row(D_MODEL),
                   pl.BlockSpec((2 * TOP_K, tm), lambda i: (0, i))),
        compiler_params=_cparams(("parallel",)),
        name="post_attention",
    )(o, x2d, p2d, wo, bo, g1, b1, wrh, wrl, br, wpg, bpg, wpp)


MOE_TILE = 512
DMA_ISSUE_UNROLL = 8


def _route_tables(expert_idx, tm):
    k, n = expert_idx.shape
    a = k * n
    n_rows = a + (N_EXPERTS + 1) * tm
    flat_e = expert_idx.reshape(a)
    order = jnp.argsort(flat_e, stable=True).astype(jnp.int32)
    cnt = jnp.sum((flat_e[:, None] == jnp.arange(N_EXPERTS, dtype=jnp.int32)[None, :]).astype(jnp.int32), axis=0)
    start = jnp.cumsum(cnt) - cnt
    pcnt = ((cnt + tm - 1) // tm) * tm
    pend = jnp.cumsum(pcnt)
    pstart = pend - pcnt
    q = jnp.arange(n_rows, dtype=jnp.int32)
    e_of_q = jnp.sum((q[:, None] >= pend[None, :]).astype(jnp.int32), axis=1)
    e_c = jnp.minimum(e_of_q, N_EXPERTS - 1)
    local = q - pstart[e_c]
    valid = (e_of_q < N_EXPERTS) & (local < cnt[e_c])
    r = jnp.clip(start[e_c] + local, 0, a - 1)
    src_tok = jnp.where(valid, order[r] % n, 0).astype(jnp.int32)
    inv = jnp.argsort(order).astype(jnp.int32)
    dest = (pstart[flat_e] + inv - start[flat_e]).astype(jnp.int32)
    nt = n_rows // tm
    tile_e = e_c[::tm]
    tile_active = (jnp.arange(nt, dtype=jnp.int32) * tm < pend[-1]).astype(jnp.int32)
    prev = jnp.concatenate([jnp.full((1,), -1, jnp.int32), tile_e[:-1]])
    tile_first = ((tile_e != prev) & (tile_active > 0)).astype(jnp.int32)
    return tile_e, tile_active, tile_first, src_tok.reshape(nt, 1, tm), dest.reshape(k, n)


def _moe_kernel(te_ref, act_ref, first_ref, tok_ref, tok_next_ref,
                x_hbm, w1_ref, b1_ref, w2_ref, b2_ref, y_ref,
                xbuf, w1b, w2b, sem):
    i = pl.program_id(0)
    tm = xbuf.shape[1] // ROW_SLABS
    slot = i % 2

    def row_copy(tref, r, s):
        src = pl.multiple_of(tref[0, 0, r] * ROW_SLABS, ROW_SLABS)
        return pltpu.make_async_copy(x_hbm.at[pl.ds(src, ROW_SLABS)], xbuf.at[s, pl.ds(r * ROW_SLABS, ROW_SLABS)], sem.at[s])

    def wait_slot(s):
        pltpu.make_async_copy(x_hbm.at[pl.ds(0, tm * ROW_SLABS)], xbuf.at[s], sem.at[s]).wait()

    active = act_ref[i] > 0
    requested = jnp.where(i == 0, act_ref[0], act_ref[jnp.maximum(i - 1, 0)]) > 0

    @pl.when((i == 0) & active)
    def _():
        def body(r, carry):
            row_copy(tok_ref, r, 0).start()
            return carry
        lax.fori_loop(0, tm, body, 0, unroll=DMA_ISSUE_UNROLL)

    @pl.when(active)
    def _():
        wait_slot(slot)

        @pl.when(first_ref[i] > 0)
        def _():
            w1b[...] = w1_ref[0].astype(BF16)
            w2b[...] = w2_ref[0].astype(BF16)

        for r in range(tm):
            row_copy(tok_next_ref, r, 1 - slot).start()

        xb = _load_row_slabs(xbuf.at[slot], 0, tm).astype(BF16)
        h = jnp.dot(xb, w1b[...], preferred_element_type=F32) + b1_ref[0]
        gate = jnp.minimum(h[:, :D_FF], SWIGLU_LIMIT)
        up = jnp.clip(h[:, D_FF:], -SWIGLU_LIMIT, SWIGLU_LIMIT)
        act = (up + 1.0) * gate * jax.nn.sigmoid(SWIGLU_ALPHA * gate)
        _store_row_slabs(y_ref, jnp.dot(act.astype(BF16), w2b[...], preferred_element_type=F32) + b2_ref[0])

    @pl.when(jnp.logical_not(active))
    def _():
        y_ref[...] = jnp.zeros_like(y_ref)

    @pl.when(requested & jnp.logical_not(active))
    def _():
        wait_slot(slot)


def routed_experts(x1, tile_e, tile_active, tile_first, src_tok, layer, w_e1, b_e1, w_e2, b_e2):
    nt, _, tm = src_tok.shape
    tokspec = lambda f: pl.BlockSpec((1, 1, tm), f, memory_space=pltpu.SMEM)
    grid_spec = pltpu.PrefetchScalarGridSpec(
        num_scalar_prefetch=3,
        grid=(nt,),
        in_specs=[tokspec(lambda i, *_: (i, 0, 0)),
                  tokspec(lambda i, *_: (jnp.minimum(i + 1, nt - 1), 0, 0)),
                  pl.BlockSpec(memory_space=pl.ANY),
                  pl.BlockSpec((None, 1, D_MODEL, 2 * D_FF), lambda i, te, *_: (layer, te[i], 0, 0)),
                  pl.BlockSpec((1, 1, 2 * D_FF), lambda i, te, *_: (te[i], 0, 0)),
                  pl.BlockSpec((None, 1, D_FF, D_MODEL), lambda i, te, *_: (layer, te[i], 0, 0)),
                  pl.BlockSpec((1, 1, D_MODEL), lambda i, te, *_: (te[i], 0, 0))],
        out_specs=pl.BlockSpec((tm * ROW_SLABS, LANES), lambda i, *_: (i, 0)),
        scratch_shapes=[pltpu.VMEM((2, tm * ROW_SLABS, LANES), F32), pltpu.VMEM((D_MODEL, 2 * D_FF), BF16),
                        pltpu.VMEM((D_FF, D_MODEL), BF16), pltpu.SemaphoreType.DMA((2,))],
    )
    return pl.pallas_call(
        _moe_kernel,
        out_shape=jax.ShapeDtypeStruct((nt * tm * ROW_SLABS, LANES), F32),
        grid_spec=grid_spec,
        compiler_params=_cparams(("arbitrary",)),
        name="routed_experts",
    )(tile_e, tile_active, tile_first, src_tok, src_tok, x1, w_e1, b_e1.reshape(N_EXPERTS, 1, 2 * D_FF),
      w_e2, b_e2.reshape(N_EXPERTS, 1, D_MODEL))


COMBINE_TILE = 128


def _combine_kernel(pos_ref, pos_next_ref, x1_ref, e_ref, gw_ref, g2_ref, b2_ref, ys_hbm, out_ref, buf, sem):
    i = pl.program_id(0)
    nt = pl.num_programs(0)
    tm = out_ref.shape[0]
    n_rows = TOP_K * tm
    slot = i % 2

    def row_copy(pref, r, s):
        src = pl.multiple_of(pref[0, 0, r] * ROW_SLABS, ROW_SLABS)
        return pltpu.make_async_copy(ys_hbm.at[pl.ds(src, ROW_SLABS)], buf.at[s, pl.ds(r * ROW_SLABS, ROW_SLABS)], sem.at[s])

    def wait_slot(s):
        pltpu.make_async_copy(ys_hbm.at[pl.ds(0, n_rows * ROW_SLABS)], buf.at[s], sem.at[s]).wait()

    @pl.when(i == 0)
    def _():
        def body(r, carry):
            row_copy(pos_ref, r, 0).start()
            return carry
        lax.fori_loop(0, n_rows, body, 0, unroll=DMA_ISSUE_UNROLL)

    wait_slot(slot)
    for r in range(n_rows):
        row_copy(pos_next_ref, r, 1 - slot).start(priority=r % 2)

    gw = gw_ref[...]
    m = jnp.zeros((tm, D_MODEL), F32)
    for k in range(TOP_K):
        m = m + gw[:, k:k + 1] * _load_row_slabs(buf.at[slot], k * tm, tm)
    out_ref[...] = _layer_norm(DN_ALPHA * _load_row_slabs(x1_ref, 0, tm) + m + e_ref[...], g2_ref[...], b2_ref[...])

    @pl.when(i == nt - 1)
    def _():
        wait_slot(1 - slot)


def combine_experts(pos, x1, e, gw, g2, b2, ys):
    n = e.shape[0]
    tm = min(n, COMBINE_TILE)
    nt = n // tm
    pos_t = pos.reshape(TOP_K, nt, tm).transpose(1, 0, 2).reshape(nt, 1, TOP_K * tm)
    posspec = lambda f: pl.BlockSpec((1, 1, TOP_K * tm), f, memory_space=pltpu.SMEM)
    row = lambda w: pl.BlockSpec((tm, w), lambda i: (i, 0))
    return pl.pallas_call(
        _combine_kernel,
        out_shape=jax.ShapeDtypeStruct((n, D_MODEL), F32),
        grid=(nt,),
        in_specs=[posspec(lambda i: (i, 0, 0)), posspec(lambda i: (jnp.minimum(i + 1, nt - 1), 0, 0)),
                  pl.BlockSpec((tm * ROW_SLABS, LANES), lambda i: (i, 0)), row(D_MODEL), row(TOP_K),
                  _full((1, D_MODEL)), _full((1, D_MODEL)), pl.BlockSpec(memory_space=pl.ANY)],
        out_specs=row(D_MODEL),
        scratch_shapes=[pltpu.VMEM((2, TOP_K * tm * ROW_SLABS, LANES), F32), pltpu.SemaphoreType.DMA((2,))],
        compiler_params=_cparams(("arbitrary",)),
        name="combine_experts",
    )(pos_t, pos_t, x1, e, gw, g2, b2, ys)


def kernel(x, p, positions, w_in, b_in, ck_pe, w_ck1, w_ck2, cv_pe, w_cv1, w_cv2, sinks, w_o, b_o,
           ln1_g, ln1_b, w_r, b_r, w_e1, b_e1, w_e2, b_e2, w_pg, b_pg, w_pp, ln2_g, ln2_b):
    batch, seq, _ = x.shape
    n = batch * seq
    n_cmp = seq // CMP_STRIDE
    inv = 1.0 / (ROPE_THETA ** (jnp.arange(0, HEAD_DIM, 2, dtype=F32) / HEAD_DIM))
    inv_tiled = jnp.tile(inv, LANES // (HEAD_DIM // 2))[None, :]
    cos, sin = rope_tables(positions.reshape(n), inv_tiled)
    pos_c = jnp.concatenate([positions[:, CMP_BLOCK - 1::CMP_STRIDE], positions[:, -1:]], axis=1)
    cos_c, sin_c = rope_tables(pos_c.reshape(batch * n_cmp), inv_tiled)
    pool = jnp.asarray(_pool_matrix(), BF16)

    x2d = x.reshape(n, D_MODEL)
    for li in range(DEPTH):
        qn, qs, ks, kw, kswa, kc, vc, vst, vwt, vswat, gates_t = input_projection(
            x2d, *_prep_inproj_weights(w_in[li], b_in[li]), cos, sin)
        kweights = _prep_compress_weights(ck_pe[li], w_ck1[li], w_ck2[li], True)
        vweights = _prep_compress_weights(cv_pe[li], w_cv1[li], w_cv2[li], False)
        kc_c, vc_ct = compress_kv(kc.reshape(batch * n_cmp, CMP_STRIDE * KV_W), vc.reshape(batch * n_cmp, CMP_STRIDE * KV_W),
                                  cos_c, sin_c, kweights, vweights)
        o = attention(sinks[li], qn, qs, gates_t, kc_c, vc_ct, ks, vst, kw, vwt, kswa, vswat, pool, batch, seq)

        wr_t = w_r[li].T
        wrh = wr_t.astype(BF16)
        wrl = (wr_t - wrh.astype(F32)).astype(BF16)
        x1, e, route = post_attention(o, x2d, p[li].reshape(n, P_DIM), w_o[li].astype(BF16), b_o[li][None, :],
                                      ln1_g[li][None, :], ln1_b[li][None, :], wrh, wrl, b_r[li][:, None],
                                      w_pg[li].astype(BF16), b_pg[li][None, :], w_pp[li].astype(BF16))
        expert_idx = route[:TOP_K].astype(jnp.int32)
        gw = route[TOP_K:].T
        tile_e, tile_active, tile_first, src_tok, pos = _route_tables(expert_idx, MOE_TILE)
        ys = routed_experts(x1, tile_e, tile_active, tile_first, src_tok, li, w_e1, b_e1[li], w_e2, b_e2[li])
        x2d = combine_experts(pos, x1, e, gw, ln2_g[li][None, :], ln2_b[li][None, :], ys)
    return x2d.reshape(batch, seq, D_MODEL)
```

```python
import numpy as np
import jax
import jax.numpy as jnp
from jax import lax
from jax.experimental import pallas as pl
from jax.experimental.pallas import tpu as pltpu

F32 = jnp.float32
BF16 = jnp.bfloat16

D_MODEL = 1024
DEPTH = 2
HEAD_DIM = 64
Q_HEADS = 8
KV_GROUPS = 2
GROUP_HEADS = Q_HEADS // KV_GROUPS
ROPE_THETA = 10000.0
CMP_STRIDE = 16
CMP_BLOCK = 32
CMP_HIDDEN = 256
SEL_BLOCK = 64
SEL_TOPN = 8
NSA_WINDOW = 512
SWA_WINDOW = 128
N_EXPERTS = 32
TOP_K = 4
D_FF = D_MODEL
SWIGLU_LIMIT = 7.0
SWIGLU_ALPHA = 1.702
P_DIM = 256
DN_ALPHA = (2 * DEPTH) ** 0.25
LN_EPS = 1e-5
NEG = -1e30
FORCE = 1e9
LOG2E = float(np.log2(np.e))

LANES = 128
Q_CHUNK = 128
SLC_UNROLL = 4
KV_W = KV_GROUPS * HEAD_DIM
Q_W = Q_HEADS * HEAD_DIM
VMEM_LIMIT = 56 * 1024 * 1024

_OFF_QN, _OFF_KC, _OFF_VC, _OFF_KS, _OFF_VS, _OFF_KW, _OFF_VW = 0, 512, 640, 768, 896, 1024, 1152
_OFF_GATES, _OFF_QS, _OFF_KSWA, _OFF_VSWA = 1280, 1304, 1816, 1944
N_GATES = Q_HEADS * 3
GATE_ROWS = 32

N_ROPE_TILES = 11
ROPE_W = N_ROPE_TILES * LANES
PROJ_W = 2 * ROPE_W + 2 * LANES
PROJ_T_ROWS = 3 * KV_W + GATE_ROWS


def _paired_head_cols(base):
    cols = []
    for j in range(GROUP_HEADS):
        for half in range(KV_GROUPS):
            h = j + GROUP_HEADS * half
            cols.extend(base + h * HEAD_DIM + d for d in range(HEAD_DIM))
    return cols


def _proj_layout():
    rope_cols = (_paired_head_cols(_OFF_QN) + _paired_head_cols(_OFF_QS)
                 + list(range(_OFF_KS, _OFF_KS + KV_W)) + list(range(_OFF_KW, _OFF_KW + KV_W))
                 + list(range(_OFF_KSWA, _OFF_KSWA + KV_W)))
    rope_cols = np.asarray(rope_cols, np.int32)
    scale = np.ones(ROPE_W, np.float64)
    scale[:2 * Q_W] = HEAD_DIM ** -0.5 * LOG2E
    pos_in_tile = np.arange(ROPE_W) % HEAD_DIM
    first_half = pos_in_tile < HEAD_DIM // 2
    rot_cols = np.where(first_half, rope_cols + HEAD_DIM // 2, rope_cols - HEAD_DIM // 2).astype(np.int32)
    rot_sign = np.where(first_half, -1.0, 1.0)
    plain_cols = np.concatenate([np.arange(o, o + KV_W) for o in (_OFF_KC, _OFF_VC)]).astype(np.int32)
    t_cols = np.concatenate([np.arange(o, o + KV_W) for o in (_OFF_VS, _OFF_VW, _OFF_VSWA)]
                            + [np.arange(_OFF_GATES, _OFF_GATES + N_GATES)]).astype(np.int32)
    return rope_cols, scale.astype(np.float32), rot_cols, (rot_sign * scale).astype(np.float32), plain_cols, t_cols


_ROPE_COLS, _ROPE_SCALE, _ROT_COLS, _ROT_SCALE, _PLAIN_COLS, _T_COLS = _proj_layout()


def _pool_matrix():
    m = np.zeros((32, LANES), np.float32)
    for c in range(LANES - 1):
        m[c // (SEL_BLOCK // CMP_STRIDE), c] = 1.0
    return m


def _cparams(sem, vmem=VMEM_LIMIT):
    return pltpu.CompilerParams(dimension_semantics=sem, vmem_limit_bytes=vmem)


def _full(shape):
    return pl.BlockSpec(shape, lambda *_: (0,) * len(shape))


_NT = (((1,), (1,)), ((), ()))


def _rope_table_kernel(pos_ref, inv_ref, cos_ref, sin_ref):
    ang = pos_ref[...] * inv_ref[...]
    cos_ref[...] = jnp.cos(ang)
    sin_ref[...] = jnp.sin(ang)


def rope_tables(pos, inv_tiled):
    m = pos.shape[0]
    pos_b = jnp.broadcast_to(pos.astype(F32)[:, None], (m, LANES))
    tm = min(m, 1024)
    spec = pl.BlockSpec((tm, LANES), lambda i: (i, 0))
    return pl.pallas_call(
        _rope_table_kernel,
        out_shape=(jax.ShapeDtypeStruct((m, LANES), F32),) * 2,
        grid=(m // tm,),
        in_specs=[spec, _full((1, LANES))],
        out_specs=(spec, spec),
        compiler_params=_cparams(("parallel",)),
        name="rope_tables",
    )(pos_b, inv_tiled)


def _inproj_kernel(x_ref, w_ref, b_ref, wt_ref, bt_ref, cos_ref, sin_ref,
                   qn_ref, qs_ref, ks_ref, kw_ref, kswa_ref, kc_ref, vc_ref,
                   vst_ref, vwt_ref, vswat_ref, gt_ref):
    xb = x_ref[...].astype(BF16)
    cos = cos_ref[...]
    sin = sin_ref[...]

    def proj(lo, hi):
        return jnp.dot(xb, w_ref[:, lo:hi], preferred_element_type=F32) + b_ref[:, lo:hi]

    def roped(tile_lo, n_tiles, out_ref):
        lo, hi = tile_lo * LANES, (tile_lo + n_tiles) * LANES
        h = proj(lo, hi)
        hr = proj(ROPE_W + lo, ROPE_W + hi)
        for t in range(n_tiles):
            sl = slice(t * LANES, (t + 1) * LANES)
            out_ref[:, sl] = (h[:, sl] * cos + hr[:, sl] * sin).astype(out_ref.dtype)

    roped(0, 4, qn_ref)
    roped(4, 4, qs_ref)
    roped(8, 1, ks_ref)
    roped(9, 1, kw_ref)
    roped(10, 1, kswa_ref)
    plain = proj(2 * ROPE_W, PROJ_W)
    kc_ref[...] = plain[:, :LANES].astype(kc_ref.dtype)
    vc_ref[...] = plain[:, LANES:].astype(vc_ref.dtype)
    tr = lax.dot_general(wt_ref[...], xb, _NT, preferred_element_type=F32) + bt_ref[...]
    for t, ref in enumerate((vst_ref, vwt_ref, vswat_ref)):
        ref[...] = tr[t * KV_W:(t + 1) * KV_W].astype(ref.dtype)
    gt_ref[...] = tr[3 * KV_W:]


def input_projection(x2d, w_all, b_all, wt, bt, cos, sin):
    n = x2d.shape[0]
    tm = min(n, 512)
    row = lambda w: pl.BlockSpec((tm, w), lambda i: (i, 0))
    col = lambda h: pl.BlockSpec((h, tm), lambda i: (0, i))
    outs = ([jax.ShapeDtypeStruct((n, Q_W), BF16)] * 2 + [jax.ShapeDtypeStruct((n, KV_W), BF16)] * 5
            + [jax.ShapeDtypeStruct((KV_W, n), BF16)] * 3 + [jax.ShapeDtypeStruct((GATE_ROWS, n), F32)])
    return pl.pallas_call(
        _inproj_kernel,
        out_shape=tuple(outs),
        grid=(n // tm,),
        in_specs=[row(D_MODEL), _full((D_MODEL, PROJ_W)), _full((1, PROJ_W)), _full((PROJ_T_ROWS, D_MODEL)),
                  _full((PROJ_T_ROWS, 1)), row(LANES), row(LANES)],
        out_specs=tuple([row(Q_W)] * 2 + [row(KV_W)] * 5 + [col(KV_W)] * 3 + [col(GATE_ROWS)]),
        compiler_params=_cparams(("parallel",)),
        name="input_projection",
    )(x2d, w_all, b_all, wt, bt, cos, sin)


def _prep_inproj_weights(w_in, b_in):
    w = jnp.concatenate([w_in[:, _ROPE_COLS] * _ROPE_SCALE, w_in[:, _ROT_COLS] * _ROT_SCALE, w_in[:, _PLAIN_COLS]], axis=1)
    b = jnp.concatenate([b_in[_ROPE_COLS] * _ROPE_SCALE, b_in[_ROT_COLS] * _ROT_SCALE, b_in[_PLAIN_COLS]])
    pad = GATE_ROWS - N_GATES
    wt = jnp.concatenate([w_in[:, _T_COLS].T, jnp.zeros((pad, D_MODEL), F32)], axis=0)
    bt = jnp.concatenate([b_in[_T_COLS], jnp.zeros((pad,), F32)])
    return w.astype(BF16), b[None, :], wt.astype(BF16), bt[:, None]


def _gelu_tanh(x):
    return 0.5 * x * (1.0 + jnp.tanh(np.sqrt(2.0 / np.pi) * (x + 0.044715 * (x * x * x))))


def _compress_kernel(kc_ref, vc_ref, cos_ref, sin_ref,
                     kw1a_ref, kw1b_ref, kpe_ref, kw2_ref, kw2r_ref,
                     vw1a_ref, vw1b_ref, vpe_ref, vw2t_ref,
                     ko_ref, vot_ref):
    def hidden(x_ref, w1a_ref, w1b_ref, pe_ref):
        x = x_ref[...]
        ya = jnp.dot(x, w1a_ref[...], preferred_element_type=F32)
        yb = jnp.dot(x, w1b_ref[...], preferred_element_type=F32)
        rows = ya.shape[0]
        h = ya + pltpu.roll(yb, rows - 1, 0)
        pe = pe_ref[...]
        peb = (jnp.dot(pe[:, :CMP_STRIDE * KV_W], w1a_ref[...], preferred_element_type=F32)
               + jnp.dot(pe[:, CMP_STRIDE * KV_W:], w1b_ref[...], preferred_element_type=F32))
        return _gelu_tanh(h + peb[0:1, :]).astype(BF16)

    ak = hidden(kc_ref, kw1a_ref, kw1b_ref, kpe_ref)
    kc = jnp.dot(ak, kw2_ref[...], preferred_element_type=F32)
    kcr = jnp.dot(ak, kw2r_ref[...], preferred_element_type=F32)
    ko_ref[...] = (kc * cos_ref[...] + kcr * sin_ref[...]).astype(ko_ref.dtype)
    av = hidden(vc_ref, vw1a_ref, vw1b_ref, vpe_ref)
    vot_ref[...] = lax.dot_general(vw2t_ref[...], av, _NT, preferred_element_type=F32).astype(vot_ref.dtype)


def _prep_compress_weights(pe, w1, w2, is_key):
    eye = jnp.eye(KV_GROUPS, dtype=F32)
    w1r = w1.reshape(CMP_BLOCK, HEAD_DIM, CMP_HIDDEN)

    def half(wh):
        return jnp.einsum("idh,ge->igdeh", wh, eye).reshape(CMP_STRIDE * KV_W, KV_GROUPS * CMP_HIDDEN).astype(BF16)

    def block_diag(w):
        wb = jnp.einsum("hd,ge->ghed", w, eye)
        return wb.reshape(KV_GROUPS * CMP_HIDDEN, KV_W)

    w1a, w1b = half(w1r[:CMP_STRIDE]), half(w1r[CMP_STRIDE:])
    pe_row = jnp.broadcast_to(pe[:, None, :], (CMP_BLOCK, KV_GROUPS, HEAD_DIM)).reshape(1, CMP_BLOCK * KV_W)
    pe_rows = jnp.broadcast_to(pe_row, (8, CMP_BLOCK * KV_W)).astype(BF16)
    if not is_key:
        return [w1a, w1b, pe_rows, block_diag(w2).T.astype(BF16)]
    half_d = HEAD_DIM // 2
    w2rot = jnp.concatenate([-w2[:, half_d:], w2[:, :half_d]], axis=1)
    return [w1a, w1b, pe_rows, block_diag(w2).astype(BF16), block_diag(w2rot).astype(BF16)]


def compress_kv(kc_chunks, vc_chunks, cos_c, sin_c, kweights, vweights):
    rows = kc_chunks.shape[0]
    tm = min(rows, 512)
    cw = CMP_STRIDE * KV_W
    hw = KV_GROUPS * CMP_HIDDEN
    row = lambda w: pl.BlockSpec((tm, w), lambda i: (i, 0))
    wspecs_k = [_full((cw, hw)), _full((cw, hw)), _full((8, 2 * cw)), _full((hw, KV_W)), _full((hw, KV_W))]
    wspecs_v = wspecs_k[:3] + [_full((KV_W, hw))]
    return pl.pallas_call(
        _compress_kernel,
        out_shape=(jax.ShapeDtypeStruct((rows, KV_W), BF16), jax.ShapeDtypeStruct((KV_W, rows), BF16)),
        grid=(rows // tm,),
        in_specs=[row(cw), row(cw), row(LANES), row(LANES)] + wspecs_k + wspecs_v,
        out_specs=(row(KV_W), pl.BlockSpec((KV_W, tm), lambda i: (0, i))),
        compiler_params=_cparams(("parallel",)),
        name="compress_kv",
    )(kc_chunks, vc_chunks, cos_c, sin_c, *kweights, *vweights)


def _attn_kernel(sinks_ref, qn_ref, qs_ref, gt_ref, kc_ref, vct_ref,
                 ks_ref, vst_ref, kw_ref, vwt_ref, ksw_ref, vswt_ref, pool_ref, o_ref,
                 sel_ref, sc_slc, sc_win, sc_swa):
    i = pl.program_id(1)
    c = Q_CHUNK
    cols = Q_HEADS * c
    key = lax.broadcasted_iota(jnp.int32, (c, c), 0)
    tq = lax.broadcasted_iota(jnp.int32, (c, c), 1)
    t_minus_key = tq - key
    lane = lax.broadcasted_iota(jnp.int32, (c, LANES), 1)
    group_mask = [jnp.where(lane < HEAD_DIM, 1.0, 0.0).astype(BF16), jnp.where(lane < HEAD_DIM, 0.0, 1.0).astype(BF16)]

    def rep(a, times=Q_HEADS):
        return jnp.concatenate([a] * times, axis=1)

    def stack_q(q_ref):
        return jnp.concatenate([q_ref[:, r * LANES:(r + 1) * LANES] * group_mask[g]
                                for g in range(KV_GROUPS) for r in range(GROUP_HEADS)], axis=0)

    sub = c // 8

    def part_max(x):
        return jnp.max(x.reshape(sub, 8, x.shape[-1]), axis=0)

    def part_sum(x):
        return jnp.sum(x.reshape(sub, 8, x.shape[-1]), axis=0)

    def score_block(q, k_blk, mask, sc_ref, row0, mpart):
        s = lax.dot_general(k_blk, q, _NT, preferred_element_type=F32)
        s = jnp.where(mask, s, NEG)
        sc_ref[pl.ds(row0, c), :] = s
        return jnp.maximum(mpart, part_max(s))

    def prob_block(vt_blk, sc_ref, row0, m, carry):
        lpart, acc = carry
        p = jnp.exp2(sc_ref[pl.ds(row0, c), :] - m)
        return lpart + part_sum(p), acc + jnp.dot(vt_blk, p.astype(BF16), preferred_element_type=F32)

    mpart0 = jnp.full((8, cols), NEG, F32)
    carry0 = (jnp.zeros((8, cols), F32), jnp.zeros((KV_W, cols), F32))

    def banded(q, k_ref, vt_ref, window, sc_ref, sink=None):
        nblk = _band_blocks(window)
        offs = []
        mpart = mpart0
        for j in range(nblk):
            back = nblk - 1 - j
            kb = i - back
            off = pl.multiple_of(jnp.maximum(kb, 0) * c, c)
            offs.append(off)
            width = jnp.where(kb >= 0, window, 0).astype(jnp.uint32)
            mask = (t_minus_key + back * c).astype(jnp.uint32) < width
            mpart = score_block(q, k_ref[pl.ds(off, c), :], rep(mask), sc_ref, j * c, mpart)
        m = jnp.max(mpart, axis=0, keepdims=True)
        if sink is not None:
            m = jnp.maximum(m, sink)
        carry = carry0
        for j in range(nblk):
            carry = prob_block(vt_ref[:, pl.ds(offs[j], c)], sc_ref, j * c, m, carry)
        l = jnp.sum(carry[0], axis=0, keepdims=True)
        if sink is not None:
            l = l + jnp.exp2(sink - m)
        return carry[1] * (1.0 / jnp.maximum(l, 1e-30))

    def compressed(q):
        s = lax.dot_general(kc_ref[...], q, _NT, preferred_element_type=F32)
        cend = key * CMP_STRIDE + (CMP_BLOCK - 1)
        mk = rep(cend <= tq + i * c)
        s = jnp.where(mk, s, NEG)
        m = jnp.max(s, axis=0, keepdims=True)
        e = jnp.where(mk, jnp.exp2(s - m), 0.0)
        p = e * (1.0 / jnp.maximum(jnp.sum(e, axis=0, keepdims=True), 1e-30))
        o = jnp.dot(vct_ref[...], p.astype(BF16), preferred_element_type=F32)
        psums = []
        for g in range(KV_GROUPS):
            base = g * GROUP_HEADS * c
            psums.append(p[:, base:base + c] + p[:, base + c:base + 2 * c]
                         + p[:, base + 2 * c:base + 3 * c] + p[:, base + 3 * c:base + 4 * c])
        return o, psums

    nsel = pool_ref.shape[0]
    blk = lax.broadcasted_iota(jnp.int32, (nsel, c), 0)
    tcol = lax.broadcasted_iota(jnp.int32, (nsel, c), 1)

    def select_blocks(psum, g):
        hi = psum.astype(BF16)
        lo = (psum - hi.astype(F32)).astype(BF16)
        pool = pool_ref[...]
        imp = jnp.dot(pool, hi, preferred_element_type=F32) + jnp.dot(pool, lo, preferred_element_type=F32)
        t = tcol + i * c
        cur = t >> 6
        forced = (blk == 0) | (blk == cur) | (blk == cur - 1)
        valid = (blk << 6) <= t
        score = jnp.where(forced, FORCE, jnp.where(valid, imp, -1.0))
        rank = jnp.zeros((nsel, c), F32)
        for k in range(nsel):
            sk = score[k:k + 1, :]
            tie = jnp.where(blk > k, 1.0, 0.0)
            rank = rank + jnp.where(sk > score, 1.0, jnp.where(sk == score, tie, 0.0))
        sel_ref[g] = jnp.where(rank < SEL_TOPN, 1.0, 0.0)

    def selected(q, sc_ref):
        nquad = (i + SLC_UNROLL) // SLC_UNROLL

        def block_mask(kb):
            causal = t_minus_key + (i - kb) * c >= 0
            per_group = []
            for g in range(KV_GROUPS):
                first = sel_ref[g, pl.ds(2 * kb, 1), :]
                second = sel_ref[g, pl.ds(2 * kb + 1, 1), :]
                chosen = jnp.where(key < SEL_BLOCK, first, second) > 0.5
                per_group.append(rep(chosen & causal, GROUP_HEADS))
            return jnp.concatenate(per_group, axis=1)

        def phase_a(qd, mpart):
            for u in range(SLC_UNROLL):
                kb = qd * SLC_UNROLL + u
                off = pl.multiple_of(kb * c, c)
                mpart = score_block(q, ks_ref[pl.ds(off, c), :], block_mask(kb), sc_ref, off, mpart)
            return mpart

        m = jnp.max(lax.fori_loop(0, nquad, phase_a, mpart0), axis=0, keepdims=True)

        def phase_b(qd, carry):
            for u in range(SLC_UNROLL):
                off = pl.multiple_of((qd * SLC_UNROLL + u) * c, c)
                carry = prob_block(vst_ref[:, pl.ds(off, c)], sc_ref, off, m, carry)
            return carry

        lpart, acc = lax.fori_loop(0, nquad, phase_b, carry0)
        return acc * (1.0 / jnp.maximum(jnp.sum(lpart, axis=0, keepdims=True), 1e-30))

    def head_block(o, h):
        g = h // GROUP_HEADS
        return o[g * HEAD_DIM:(g + 1) * HEAD_DIM, h * c:(h + 1) * c]

    gs = jax.nn.sigmoid(gt_ref[...])
    q = stack_q(qn_ref)
    o_cmp, psums = compressed(q)
    for g in range(KV_GROUPS):
        select_blocks(psums[g], g)
    o_slc = selected(q, sc_slc)
    o_win = banded(q, kw_ref, vwt_ref, NSA_WINDOW, sc_win)
    heads = [gs[3 * h:3 * h + 1] * head_block(o_cmp, h) + gs[3 * h + 1:3 * h + 2] * head_block(o_slc, h)
             + gs[3 * h + 2:3 * h + 3] * head_block(o_win, h) for h in range(Q_HEADS)]

    head_col = lax.broadcasted_iota(jnp.int32, (1, cols), 1) // c
    sink = jnp.zeros((1, cols), F32)
    for h in range(Q_HEADS):
        sink = jnp.where(head_col == h, sinks_ref[h] * LOG2E, sink)
    o_swa = banded(stack_q(qs_ref), ksw_ref, vswt_ref, SWA_WINDOW, sc_swa, sink)
    heads += [head_block(o_swa, h) for h in range(Q_HEADS)]
    o_ref[...] = jnp.concatenate(heads, axis=0).T.astype(o_ref.dtype)


def _band_blocks(window):
    return (window - 1 + Q_CHUNK - 1) // Q_CHUNK + 1


def attention(sinks, qn, qs, gates_t, kc_c, vc_ct, ks, vst, kw, vwt, ksw, vswt, pool, batch, seq):
    n = batch * seq
    nq = seq // Q_CHUNK
    assert seq // SEL_BLOCK == pool.shape[0] and seq // CMP_STRIDE == LANES
    qspec = lambda w: pl.BlockSpec((Q_CHUNK, w), lambda b, i, *_: (b * nq + i, 0))
    kspec = pl.BlockSpec((seq, KV_W), lambda b, i, *_: (b, 0))
    vtspec = pl.BlockSpec((KV_W, seq), lambda b, i, *_: (0, b))
    grid_spec = pltpu.PrefetchScalarGridSpec(
        num_scalar_prefetch=1,
        grid=(batch, nq),
        in_specs=[qspec(Q_W), qspec(Q_W), pl.BlockSpec((GATE_ROWS, Q_CHUNK), lambda b, i, *_: (0, b * nq + i)),
                  pl.BlockSpec((LANES, KV_W), lambda b, i, *_: (b, 0)), pl.BlockSpec((KV_W, LANES), lambda b, i, *_: (0, b)),
                  kspec, vtspec, kspec, vtspec, kspec, vtspec, _full(pool.shape)],
        out_specs=qspec(2 * Q_W),
        scratch_shapes=[pltpu.VMEM((KV_GROUPS,) + pool.shape[:1] + (Q_CHUNK,), F32),
                        pltpu.VMEM((seq, Q_HEADS * Q_CHUNK), F32),
                        pltpu.VMEM((_band_blocks(NSA_WINDOW) * Q_CHUNK, Q_HEADS * Q_CHUNK), F32),
                        pltpu.VMEM((_band_blocks(SWA_WINDOW) * Q_CHUNK, Q_HEADS * Q_CHUNK), F32)],
    )
    return pl.pallas_call(
        _attn_kernel,
        out_shape=jax.ShapeDtypeStruct((n, 2 * Q_W), BF16),
        grid_spec=grid_spec,
        compiler_params=_cparams(("parallel", "arbitrary")),
        name="attention",
    )(sinks, qn, qs, gates_t, kc_c, vc_ct, ks, vst, kw, vwt, ksw, vswt, pool)


ROW_SLABS = D_MODEL // LANES


def _store_row_slabs(ref, val):
    rows = val.shape[0]
    for s in range(ROW_SLABS):
        ref[pl.ds(s, rows, stride=ROW_SLABS), :] = val[:, s * LANES:(s + 1) * LANES]


def _load_row_slabs(ref, first_row, rows):
    return jnp.concatenate([ref[pl.ds(first_row * ROW_SLABS + s, rows, stride=ROW_SLABS), :]
                            for s in range(ROW_SLABS)], axis=1)


def _layer_norm(y, g, b):
    mu = jnp.mean(y, axis=-1, keepdims=True)
    yc = y - mu
    var = jnp.mean(yc * yc, axis=-1, keepdims=True)
    return yc * lax.rsqrt(var + LN_EPS) * g + b


def _post_attn_kernel(o_ref, x_ref, p_ref, wo_ref, bo_ref, g1_ref, b1_ref,
                      wrh_ref, wrl_ref, br_ref, wpg_ref, bpg_ref, wpp_ref,
                      x1_ref, e_ref, route_ref):
    a = jnp.dot(o_ref[...], wo_ref[...], preferred_element_type=F32) + bo_ref[...]
    x1 = _layer_norm(DN_ALPHA * x_ref[...] + a, g1_ref[...], b1_ref[...])
    _store_row_slabs(x1_ref, x1)
    xh = x1.astype(BF16)
    gate = jax.nn.sigmoid(jnp.dot(xh, wpg_ref[...], preferred_element_type=F32) + bpg_ref[...])
    e_ref[...] = gate * jnp.dot(p_ref[...].astype(BF16), wpp_ref[...], preferred_element_type=F32)
    xl = (x1 - xh.astype(F32)).astype(BF16)
    logits = (lax.dot_general(wrh_ref[...], xh, _NT, preferred_element_type=F32)
              + lax.dot_general(wrh_ref[...], xl, _NT, preferred_element_type=F32)
              + lax.dot_general(wrl_ref[...], xh, _NT, preferred_element_type=F32)) + br_ref[...]
    eidx = lax.broadcasted_iota(jnp.int32, logits.shape, 0).astype(F32)
    vals, idxs = [], []
    for _ in range(TOP_K):
        v = jnp.max(logits, axis=0, keepdims=True)
        ix = jnp.min(jnp.where(logits == v, eidx, float(N_EXPERTS)), axis=0, keepdims=True)
        logits = jnp.where(eidx == ix, -jnp.inf, logits)
        vals.append(v)
        idxs.append(ix)
    exps = [jnp.exp(v - vals[0]) for v in vals]
    den = exps[0] + exps[1] + exps[2] + exps[3]
    route_ref[...] = jnp.concatenate(idxs + [ex / den for ex in exps], axis=0)


def post_attention(o, x2d, p2d, wo, bo, g1, b1, wrh, wrl, br, wpg, bpg, wpp):
    n = x2d.shape[0]
    tm = min(n, 512)
    row = lambda w: pl.BlockSpec((tm, w), lambda i: (i, 0))
    return pl.pallas_call(
        _post_attn_kernel,
        out_shape=(jax.ShapeDtypeStruct((n * ROW_SLABS, LANES), F32), jax.ShapeDtypeStruct((n, D_MODEL), F32),
                   jax.ShapeDtypeStruct((2 * TOP_K, n), F32)),
        grid=(n // tm,),
        in_specs=[row(D_MODEL), row(D_MODEL), row(P_DIM), _full((D_MODEL, D_MODEL)), _full((1, D_MODEL)),
                  _full((1, D_MODEL)), _full((1, D_MODEL)), _full((N_EXPERTS, D_MODEL)), _full((N_EXPERTS, D_MODEL)),
                  _full((N_EXPERTS, 1)), _full((D_MODEL, D_MODEL)), _full((1, D_MODEL)), _full((P_DIM, D_MODEL))],
        out_specs=(pl.BlockSpec((tm * ROW_SLABS, LANES), lambda i: (i, 0)), row(D_MODEL),
                   pl.BlockSpec((2 * TOP_K, tm), lambda i: (0, i))),
        compiler_params=_cparams(("parallel",)),
        name="post_attention",
    )(o, x2d, p2d, wo, bo, g1, b1, wrh, wrl, br, wpg, bpg, wpp)


MOE_TILE = 256
MOE_K_CHUNKS = 4
DMA_ISSUE_UNROLL = 8


def _route_tables(expert_idx, tm):
    k, n = expert_idx.shape
    a = k * n
    n_rows = a + (N_EXPERTS + 1) * tm
    flat_e = expert_idx.reshape(a)
    order = jnp.argsort(flat_e, stable=True).astype(jnp.int32)
    cnt = jnp.sum((flat_e[:, None] == jnp.arange(N_EXPERTS, dtype=jnp.int32)[None, :]).astype(jnp.int32), axis=0)
    start = jnp.cumsum(cnt) - cnt
    pcnt = ((cnt + tm - 1) // tm) * tm
    pend = jnp.cumsum(pcnt)
    pstart = pend - pcnt
    q = jnp.arange(n_rows, dtype=jnp.int32)
    e_of_q = jnp.sum((q[:, None] >= pend[None, :]).astype(jnp.int32), axis=1)
    e_c = jnp.minimum(e_of_q, N_EXPERTS - 1)
    local = q - pstart[e_c]
    valid = (e_of_q < N_EXPERTS) & (local < cnt[e_c])
    r = jnp.clip(start[e_c] + local, 0, a - 1)
    src_tok = jnp.where(valid, order[r] % n, 0).astype(jnp.int32)
    inv = jnp.argsort(order).astype(jnp.int32)
    dest = (pstart[flat_e] + inv - start[flat_e]).astype(jnp.int32)
    nt = n_rows // tm
    tile_e = e_c[::tm]
    tile_active = (jnp.arange(nt, dtype=jnp.int32) * tm < pend[-1]).astype(jnp.int32)
    prev = jnp.concatenate([jnp.full((1,), -1, jnp.int32), tile_e[:-1]])
    tile_first = ((tile_e != prev) & (tile_active > 0)).astype(jnp.int32)
    return tile_e, tile_active, tile_first, src_tok.reshape(nt, 1, tm), dest.reshape(k, n)


def _moe_kernel(te_ref, act_ref, first_ref, tok_ref, tok_next_ref,
                x_hbm, w1_ref, b1_ref, w2_ref, b2_ref, y_ref,
                xbuf, w1b, w2b, sem):
    i = pl.program_id(0)
    tm = xbuf.shape[1] // ROW_SLABS
    slot = i % 2

    def row_copy(tref, r, s):
        src = pl.multiple_of(tref[0, 0, r] * ROW_SLABS, ROW_SLABS)
        return pltpu.make_async_copy(x_hbm.at[pl.ds(src, ROW_SLABS)], xbuf.at[s, pl.ds(r * ROW_SLABS, ROW_SLABS)], sem.at[s])

    def wait_slot(s):
        pltpu.make_async_copy(x_hbm.at[pl.ds(0, tm * ROW_SLABS)], xbuf.at[s], sem.at[s]).wait()

    active = act_ref[i] > 0
    requested = jnp.where(i == 0, act_ref[0], act_ref[jnp.maximum(i - 1, 0)]) > 0

    @pl.when((i == 0) & active)
    def _():
        def body(r, carry):
            row_copy(tok_ref, r, 0).start()
            return carry
        lax.fori_loop(0, tm, body, 0, unroll=DMA_ISSUE_UNROLL)

    @pl.when(active)
    def _():
        wait_slot(slot)

        @pl.when(first_ref[i] > 0)
        def _():
            w1b[...] = w1_ref[0].astype(BF16)
            w2b[...] = w2_ref[0].astype(BF16)

        per = tm // MOE_K_CHUNKS
        kw = D_MODEL // MOE_K_CHUNKS
        h = jnp.zeros((tm, 2 * D_FF), F32) + b1_ref[0]
        for kc in range(MOE_K_CHUNKS):
            for r in range(kc * per, (kc + 1) * per):
                row_copy(tok_next_ref, r, 1 - slot).start(priority=r % 2)
            xk = jnp.concatenate([xbuf[slot, pl.ds(s, tm, stride=ROW_SLABS), :]
                                  for s in range(kc * kw // LANES, (kc + 1) * kw // LANES)], axis=1).astype(BF16)
            h = h + jnp.dot(xk, w1b[kc * kw:(kc + 1) * kw, :], preferred_element_type=F32)
        gate = jnp.minimum(h[:, :D_FF], SWIGLU_LIMIT)
        up = jnp.clip(h[:, D_FF:], -SWIGLU_LIMIT, SWIGLU_LIMIT)
        act = (up + 1.0) * gate * jax.nn.sigmoid(SWIGLU_ALPHA * gate)
        _store_row_slabs(y_ref, jnp.dot(act.astype(BF16), w2b[...], preferred_element_type=F32) + b2_ref[0])

    @pl.when(jnp.logical_not(active))
    def _():
        y_ref[...] = jnp.zeros_like(y_ref)

    @pl.when(requested & jnp.logical_not(active))
    def _():
        wait_slot(slot)


def routed_experts(x1, tile_e, tile_active, tile_first, src_tok, layer, w_e1, b_e1, w_e2, b_e2):
    nt, _, tm = src_tok.shape
    tokspec = lambda f: pl.BlockSpec((1, 1, tm), f, memory_space=pltpu.SMEM)
    grid_spec = pltpu.PrefetchScalarGridSpec(
        num_scalar_prefetch=3,
        grid=(nt,),
        in_specs=[tokspec(lambda i, *_: (i, 0, 0)),
                  tokspec(lambda i, *_: (jnp.minimum(i + 1, nt - 1), 0, 0)),
                  pl.BlockSpec(memory_space=pl.ANY),
                  pl.BlockSpec((None, 1, D_MODEL, 2 * D_FF), lambda i, te, *_: (layer, te[i], 0, 0)),
                  pl.BlockSpec((1, 1, 2 * D_FF), lambda i, te, *_: (te[i], 0, 0)),
                  pl.BlockSpec((None, 1, D_FF, D_MODEL), lambda i, te, *_: (layer, te[i], 0, 0)),
                  pl.BlockSpec((1, 1, D_MODEL), lambda i, te, *_: (te[i], 0, 0))],
        out_specs=pl.BlockSpec((tm * ROW_SLABS, LANES), lambda i, *_: (i, 0)),
        scratch_shapes=[pltpu.VMEM((2, tm * ROW_SLABS, LANES), F32), pltpu.VMEM((D_MODEL, 2 * D_FF), BF16),
                        pltpu.VMEM((D_FF, D_MODEL), BF16), pltpu.SemaphoreType.DMA((2,))],
    )
    return pl.pallas_call(
        _moe_kernel,
        out_shape=jax.ShapeDtypeStruct((nt * tm * ROW_SLABS, LANES), F32),
        grid_spec=grid_spec,
        compiler_params=_cparams(("arbitrary",)),
        name="routed_experts",
    )(tile_e, tile_active, tile_first, src_tok, src_tok, x1, w_e1, b_e1.reshape(N_EXPERTS, 1, 2 * D_FF),
      w_e2, b_e2.reshape(N_EXPERTS, 1, D_MODEL))


COMBINE_TILE = 128


def _combine_kernel(pos_ref, pos_next_ref, x1_ref, e_ref, gw_ref, g2_ref, b2_ref, ys_hbm, out_ref, buf, sem):
    i = pl.program_id(0)
    nt = pl.num_programs(0)
    tm = out_ref.shape[0]
    n_rows = TOP_K * tm
    slot = i % 2

    def row_copy(pref, r, s):
        src = pl.multiple_of(pref[0, 0, r] * ROW_SLABS, ROW_SLABS)
        return pltpu.make_async_copy(ys_hbm.at[pl.ds(src, ROW_SLABS)], buf.at[s, pl.ds(r * ROW_SLABS, ROW_SLABS)], sem.at[s])

    def wait_slot(s):
        pltpu.make_async_copy(ys_hbm.at[pl.ds(0, n_rows * ROW_SLABS)], buf.at[s], sem.at[s]).wait()

    @pl.when(i == 0)
    def _():
        def body(r, carry):
            row_copy(pos_ref, r, 0).start()
            return carry
        lax.fori_loop(0, n_rows, body, 0, unroll=DMA_ISSUE_UNROLL)

    wait_slot(slot)
    for r in range(n_rows):
        row_copy(pos_next_ref, r, 1 - slot).start(priority=r % 2)

    gw = gw_ref[...]
    m = jnp.zeros((tm, D_MODEL), F32)
    for k in range(TOP_K):
        m = m + gw[:, k:k + 1] * _load_row_slabs(buf.at[slot], k * tm, tm)
    out_ref[...] = _layer_norm(DN_ALPHA * _load_row_slabs(x1_ref, 0, tm) + m + e_ref[...], g2_ref[...], b2_ref[...])

    @pl.when(i == nt - 1)
    def _():
        wait_slot(1 - slot)


def combine_experts(pos, x1, e, gw, g2, b2, ys):
    n = e.shape[0]
    tm = min(n, COMBINE_TILE)
    nt = n // tm
    pos_t = pos.reshape(TOP_K, nt, tm).transpose(1, 0, 2).reshape(nt, 1, TOP_K * tm)
    posspec = lambda f: pl.BlockSpec((1, 1, TOP_K * tm), f, memory_space=pltpu.SMEM)
    row = lambda w: pl.BlockSpec((tm, w), lambda i: (i, 0))
    return pl.pallas_call(
        _combine_kernel,
        out_shape=jax.ShapeDtypeStruct((n, D_MODEL), F32),
        grid=(nt,),
        in_specs=[posspec(lambda i: (i, 0, 0)), posspec(lambda i: (jnp.minimum(i + 1, nt - 1), 0, 0)),
                  pl.BlockSpec((tm * ROW_SLABS, LANES), lambda i: (i, 0)), row(D_MODEL), row(TOP_K),
                  _full((1, D_MODEL)), _full((1, D_MODEL)), pl.BlockSpec(memory_space=pl.ANY)],
        out_specs=row(D_MODEL),
        scratch_shapes=[pltpu.VMEM((2, TOP_K * tm * ROW_SLABS, LANES), F32), pltpu.SemaphoreType.DMA((2,))],
        compiler_params=_cparams(("arbitrary",)),
        name="combine_experts",
    )(pos_t, pos_t, x1, e, gw, g2, b2, ys)


def kernel(x, p, positions, w_in, b_in, ck_pe, w_ck1, w_ck2, cv_pe, w_cv1, w_cv2, sinks, w_o, b_o,
           ln1_g, ln1_b, w_r, b_r, w_e1, b_e1, w_e2, b_e2, w_pg, b_pg, w_pp, ln2_g, ln2_b):
    batch, seq, _ = x.shape
    n = batch * seq
    n_cmp = seq // CMP_STRIDE
    inv = 1.0 / (ROPE_THETA ** (jnp.arange(0, HEAD_DIM, 2, dtype=F32) / HEAD_DIM))
    inv_tiled = jnp.tile(inv, LANES // (HEAD_DIM // 2))[None, :]
    cos, sin = rope_tables(positions.reshape(n), inv_tiled)
    pos_c = jnp.concatenate([positions[:, CMP_BLOCK - 1::CMP_STRIDE], positions[:, -1:]], axis=1)
    cos_c, sin_c = rope_tables(pos_c.reshape(batch * n_cmp), inv_tiled)
    pool = jnp.asarray(_pool_matrix(), BF16)

    x2d = x.reshape(n, D_MODEL)
    for li in range(DEPTH):
        qn, qs, ks, kw, kswa, kc, vc, vst, vwt, vswat, gates_t = input_projection(
            x2d, *_prep_inproj_weights(w_in[li], b_in[li]), cos, sin)
        kweights = _prep_compress_weights(ck_pe[li], w_ck1[li], w_ck2[li], True)
        vweights = _prep_compress_weights(cv_pe[li], w_cv1[li], w_cv2[li], False)
        kc_c, vc_ct = compress_kv(kc.reshape(batch * n_cmp, CMP_STRIDE * KV_W), vc.reshape(batch * n_cmp, CMP_STRIDE * KV_W),
                                  cos_c, sin_c, kweights, vweights)
        o = attention(sinks[li], qn, qs, gates_t, kc_c, vc_ct, ks, vst, kw, vwt, kswa, vswat, pool, batch, seq)

        wr_t = w_r[li].T
        wrh = wr_t.astype(BF16)
        wrl = (wr_t - wrh.astype(F32)).astype(BF16)
        x1, e, route = post_attention(o, x2d, p[li].reshape(n, P_DIM), w_o[li].astype(BF16), b_o[li][None, :],
                                      ln1_g[li][None, :], ln1_b[li][None, :], wrh, wrl, b_r[li][:, None],
                                      w_pg[li].astype(BF16), b_pg[li][None, :], w_pp[li].astype(BF16))
        expert_idx = route[:TOP_K].astype(jnp.int32)
        gw = route[TOP_K:].T
        tile_e, tile_active, tile_first, src_tok, pos = _route_tables(expert_idx, MOE_TILE)
        ys = routed_experts(x1, tile_e, tile_active, tile_first, src_tok, li, w_e1, b_e1[li], w_e2, b_e2[li])
        x2d = combine_experts(pos, x1, e, gw, ln2_g[li][None, :], ln2_b[li][None, :], ys)
    return x2d.reshape(batch, seq, D_MODEL)
```

```python
import numpy as np
import jax
import jax.numpy as jnp
from jax import lax
from jax.experimental import pallas as pl
from jax.experimental.pallas import tpu as pltpu

F32 = jnp.float32
BF16 = jnp.bfloat16

D_MODEL = 1024
DEPTH = 2
HEAD_DIM = 64
Q_HEADS = 8
KV_GROUPS = 2
GROUP_HEADS = Q_HEADS // KV_GROUPS
ROPE_THETA = 10000.0
CMP_STRIDE = 16
CMP_BLOCK = 32
CMP_HIDDEN = 256
SEL_BLOCK = 64
SEL_TOPN = 8
NSA_WINDOW = 512
SWA_WINDOW = 128
N_EXPERTS = 32
TOP_K = 4
D_FF = D_MODEL
SWIGLU_LIMIT = 7.0
SWIGLU_ALPHA = 1.702
P_DIM = 256
DN_ALPHA = (2 * DEPTH) ** 0.25
LN_EPS = 1e-5
NEG = -1e30
FORCE = 1e9
LOG2E = float(np.log2(np.e))

LANES = 128
Q_CHUNK = 128
SLC_UNROLL = 4
KV_W = KV_GROUPS * HEAD_DIM
Q_W = Q_HEADS * HEAD_DIM
VMEM_LIMIT = 56 * 1024 * 1024

_OFF_QN, _OFF_KC, _OFF_VC, _OFF_KS, _OFF_VS, _OFF_KW, _OFF_VW = 0, 512, 640, 768, 896, 1024, 1152
_OFF_GATES, _OFF_QS, _OFF_KSWA, _OFF_VSWA = 1280, 1304, 1816, 1944
N_GATES = Q_HEADS * 3
GATE_ROWS = 32

N_ROPE_TILES = 11
ROPE_W = N_ROPE_TILES * LANES
PROJ_W = 2 * ROPE_W + 2 * LANES
PROJ_T_ROWS = 3 * KV_W + GATE_ROWS


def _paired_head_cols(base):
    cols = []
    for j in range(GROUP_HEADS):
        for half in range(KV_GROUPS):
            h = j + GROUP_HEADS * half
            cols.extend(base + h * HEAD_DIM + d for d in range(HEAD_DIM))
    return cols


def _proj_layout():
    rope_cols = (_paired_head_cols(_OFF_QN) + _paired_head_cols(_OFF_QS)
                 + list(range(_OFF_KS, _OFF_KS + KV_W)) + list(range(_OFF_KW, _OFF_KW + KV_W))
                 + list(range(_OFF_KSWA, _OFF_KSWA + KV_W)))
    rope_cols = np.asarray(rope_cols, np.int32)
    scale = np.ones(ROPE_W, np.float64)
    scale[:2 * Q_W] = HEAD_DIM ** -0.5 * LOG2E
    pos_in_tile = np.arange(ROPE_W) % HEAD_DIM
    first_half = pos_in_tile < HEAD_DIM // 2
    rot_cols = np.where(first_half, rope_cols + HEAD_DIM // 2, rope_cols - HEAD_DIM // 2).astype(np.int32)
    rot_sign = np.where(first_half, -1.0, 1.0)
    plain_cols = np.concatenate([np.arange(o, o + KV_W) for o in (_OFF_KC, _OFF_VC)]).astype(np.int32)
    t_cols = np.concatenate([np.arange(o, o + KV_W) for o in (_OFF_VS, _OFF_VW, _OFF_VSWA)]
                            + [np.arange(_OFF_GATES, _OFF_GATES + N_GATES)]).astype(np.int32)
    return rope_cols, scale.astype(np.float32), rot_cols, (rot_sign * scale).astype(np.float32), plain_cols, t_cols


_ROPE_COLS, _ROPE_SCALE, _ROT_COLS, _ROT_SCALE, _PLAIN_COLS, _T_COLS = _proj_layout()


def _pool_matrix():
    m = np.zeros((32, LANES), np.float32)
    for c in range(LANES - 1):
        m[c // (SEL_BLOCK // CMP_STRIDE), c] = 1.0
    return m


def _cparams(sem, vmem=VMEM_LIMIT):
    return pltpu.CompilerParams(dimension_semantics=sem, vmem_limit_bytes=vmem)


def _full(shape):
    return pl.BlockSpec(shape, lambda *_: (0,) * len(shape))


_NT = (((1,), (1,)), ((), ()))


def _rope_table_kernel(pos_ref, inv_ref, cos_ref, sin_ref):
    ang = pos_ref[...] * inv_ref[...]
    cos_ref[...] = jnp.cos(ang)
    sin_ref[...] = jnp.sin(ang)


def rope_tables(pos, inv_tiled):
    m = pos.shape[0]
    pos_b = jnp.broadcast_to(pos.astype(F32)[:, None], (m, LANES))
    tm = min(m, 1024)
    spec = pl.BlockSpec((tm, LANES), lambda i: (i, 0))
    return pl.pallas_call(
        _rope_table_kernel,
        out_shape=(jax.ShapeDtypeStruct((m, LANES), F32),) * 2,
        grid=(m // tm,),
        in_specs=[spec, _full((1, LANES))],
        out_specs=(spec, spec),
        compiler_params=_cparams(("parallel",)),
        name="rope_tables",
    )(pos_b, inv_tiled)


def _inproj_kernel(x_ref, w_ref, b_ref, wt_ref, bt_ref, cos_ref, sin_ref,
                   qn_ref, qs_ref, ks_ref, kw_ref, kswa_ref, kc_ref, vc_ref,
                   vst_ref, vwt_ref, vswat_ref, gt_ref):
    xb = x_ref[...].astype(BF16)
    cos = cos_ref[...]
    sin = sin_ref[...]

    def proj(lo, hi):
        return jnp.dot(xb, w_ref[:, lo:hi], preferred_element_type=F32) + b_ref[:, lo:hi]

    def roped(tile_lo, n_tiles, out_ref):
        lo, hi = tile_lo * LANES, (tile_lo + n_tiles) * LANES
        h = proj(lo, hi)
        hr = proj(ROPE_W + lo, ROPE_W + hi)
        for t in range(n_tiles):
            sl = slice(t * LANES, (t + 1) * LANES)
            out_ref[:, sl] = (h[:, sl] * cos + hr[:, sl] * sin).astype(out_ref.dtype)

    roped(0, 4, qn_ref)
    roped(4, 4, qs_ref)
    roped(8, 1, ks_ref)
    roped(9, 1, kw_ref)
    roped(10, 1, kswa_ref)
    plain = proj(2 * ROPE_W, PROJ_W)
    kc_ref[...] = plain[:, :LANES].astype(kc_ref.dtype)
    vc_ref[...] = plain[:, LANES:].astype(vc_ref.dtype)
    tr = lax.dot_general(wt_ref[...], xb, _NT, preferred_element_type=F32) + bt_ref[...]
    for t, ref in enumerate((vst_ref, vwt_ref, vswat_ref)):
        ref[...] = tr[t * KV_W:(t + 1) * KV_W].astype(ref.dtype)
    gt_ref[...] = tr[3 * KV_W:]


def input_projection(x2d, w_all, b_all, wt, bt, cos, sin):
    n = x2d.shape[0]
    tm = min(n, 512)
    row = lambda w: pl.BlockSpec((tm, w), lambda i: (i, 0))
    col = lambda h: pl.BlockSpec((h, tm), lambda i: (0, i))
    outs = ([jax.ShapeDtypeStruct((n, Q_W), BF16)] * 2 + [jax.ShapeDtypeStruct((n, KV_W), BF16)] * 5
            + [jax.ShapeDtypeStruct((KV_W, n), BF16)] * 3 + [jax.ShapeDtypeStruct((GATE_ROWS, n), F32)])
    return pl.pallas_call(
        _inproj_kernel,
        out_shape=tuple(outs),
        grid=(n // tm,),
        in_specs=[row(D_MODEL), _full((D_MODEL, PROJ_W)), _full((1, PROJ_W)), _full((PROJ_T_ROWS, D_MODEL)),
                  _full((PROJ_T_ROWS, 1)), row(LANES), row(LANES)],
        out_specs=tuple([row(Q_W)] * 2 + [row(KV_W)] * 5 + [col(KV_W)] * 3 + [col(GATE_ROWS)]),
        compiler_params=_cparams(("parallel",)),
        name="input_projection",
    )(x2d, w_all, b_all, wt, bt, cos, sin)


def _prep_inproj_weights(w_in, b_in):
    w = jnp.concatenate([w_in[:, _ROPE_COLS] * _ROPE_SCALE, w_in[:, _ROT_COLS] * _ROT_SCALE, w_in[:, _PLAIN_COLS]], axis=1)
    b = jnp.concatenate([b_in[_ROPE_COLS] * _ROPE_SCALE, b_in[_ROT_COLS] * _ROT_SCALE, b_in[_PLAIN_COLS]])
    pad = GATE_ROWS - N_GATES
    wt = jnp.concatenate([w_in[:, _T_COLS].T, jnp.zeros((pad, D_MODEL), F32)], axis=0)
    bt = jnp.concatenate([b_in[_T_COLS], jnp.zeros((pad,), F32)])
    return w.astype(BF16), b[None, :], wt.astype(BF16), bt[:, None]


def _gelu_tanh(x):
    return 0.5 * x * (1.0 + jnp.tanh(np.sqrt(2.0 / np.pi) * (x + 0.044715 * (x * x * x))))


def _compress_kernel(kc_ref, vc_ref, cos_ref, sin_ref,
                     kw1a_ref, kw1b_ref, kpe_ref, kw2_ref, kw2r_ref,
                     vw1a_ref, vw1b_ref, vpe_ref, vw2t_ref,
                     ko_ref, vot_ref):
    def hidden(x_ref, w1a_ref, w1b_ref, pe_ref):
        x = x_ref[...]
        ya = jnp.dot(x, w1a_ref[...], preferred_element_type=F32)
        yb = jnp.dot(x, w1b_ref[...], preferred_element_type=F32)
        rows = ya.shape[0]
        h = ya + pltpu.roll(yb, rows - 1, 0)
        pe = pe_ref[...]
        peb = (jnp.dot(pe[:, :CMP_STRIDE * KV_W], w1a_ref[...], preferred_element_type=F32)
               + jnp.dot(pe[:, CMP_STRIDE * KV_W:], w1b_ref[...], preferred_element_type=F32))
        return _gelu_tanh(h + peb[0:1, :]).astype(BF16)

    ak = hidden(kc_ref, kw1a_ref, kw1b_ref, kpe_ref)
    kc = jnp.dot(ak, kw2_ref[...], preferred_element_type=F32)
    kcr = jnp.dot(ak, kw2r_ref[...], preferred_element_type=F32)
    ko_ref[...] = (kc * cos_ref[...] + kcr * sin_ref[...]).astype(ko_ref.dtype)
    av = hidden(vc_ref, vw1a_ref, vw1b_ref, vpe_ref)
    vot_ref[...] = lax.dot_general(vw2t_ref[...], av, _NT, preferred_element_type=F32).astype(vot_ref.dtype)


def _prep_compress_weights(pe, w1, w2, is_key):
    eye = jnp.eye(KV_GROUPS, dtype=F32)
    w1r = w1.reshape(CMP_BLOCK, HEAD_DIM, CMP_HIDDEN)

    def half(wh):
        return jnp.einsum("idh,ge->igdeh", wh, eye).reshape(CMP_STRIDE * KV_W, KV_GROUPS * CMP_HIDDEN).astype(BF16)

    def block_diag(w):
        wb = jnp.einsum("hd,ge->ghed", w, eye)
        return wb.reshape(KV_GROUPS * CMP_HIDDEN, KV_W)

    w1a, w1b = half(w1r[:CMP_STRIDE]), half(w1r[CMP_STRIDE:])
    pe_row = jnp.broadcast_to(pe[:, None, :], (CMP_BLOCK, KV_GROUPS, HEAD_DIM)).reshape(1, CMP_BLOCK * KV_W)
    pe_rows = jnp.broadcast_to(pe_row, (8, CMP_BLOCK * KV_W)).astype(BF16)
    if not is_key:
        return [w1a, w1b, pe_rows, block_diag(w2).T.astype(BF16)]
    half_d = HEAD_DIM // 2
    w2rot = jnp.concatenate([-w2[:, half_d:], w2[:, :half_d]], axis=1)
    return [w1a, w1b, pe_rows, block_diag(w2).astype(BF16), block_diag(w2rot).astype(BF16)]


def compress_kv(kc_chunks, vc_chunks, cos_c, sin_c, kweights, vweights):
    rows = kc_chunks.shape[0]
    tm = min(rows, 512)
    cw = CMP_STRIDE * KV_W
    hw = KV_GROUPS * CMP_HIDDEN
    row = lambda w: pl.BlockSpec((tm, w), lambda i: (i, 0))
    wspecs_k = [_full((cw, hw)), _full((cw, hw)), _full((8, 2 * cw)), _full((hw, KV_W)), _full((hw, KV_W))]
    wspecs_v = wspecs_k[:3] + [_full((KV_W, hw))]
    return pl.pallas_call(
        _compress_kernel,
        out_shape=(jax.ShapeDtypeStruct((rows, KV_W), BF16), jax.ShapeDtypeStruct((KV_W, rows), BF16)),
        grid=(rows // tm,),
        in_specs=[row(cw), row(cw), row(LANES), row(LANES)] + wspecs_k + wspecs_v,
        out_specs=(row(KV_W), pl.BlockSpec((KV_W, tm), lambda i: (0, i))),
        compiler_params=_cparams(("parallel",)),
        name="compress_kv",
    )(kc_chunks, vc_chunks, cos_c, sin_c, *kweights, *vweights)


def _attn_kernel(sinks_ref, qn_ref, qs_ref, gt_ref, kc_ref, vct_ref,
                 ks_ref, vst_ref, kw_ref, vwt_ref, ksw_ref, vswt_ref, pool_ref, o_ref,
                 sel_ref, sc_slc, sc_win, sc_swa):
    i = pl.program_id(1)
    c = Q_CHUNK
    cols = Q_HEADS * c
    key = lax.broadcasted_iota(jnp.int32, (c, c), 0)
    tq = lax.broadcasted_iota(jnp.int32, (c, c), 1)
    t_minus_key = tq - key
    lane = lax.broadcasted_iota(jnp.int32, (c, LANES), 1)
    group_mask = [jnp.where(lane < HEAD_DIM, 1.0, 0.0).astype(BF16), jnp.where(lane < HEAD_DIM, 0.0, 1.0).astype(BF16)]

    def rep(a, times=Q_HEADS):
        return jnp.concatenate([a] * times, axis=1)

    def stack_q(q_ref):
        return jnp.concatenate([q_ref[:, r * LANES:(r + 1) * LANES] * group_mask[g]
                                for g in range(KV_GROUPS) for r in range(GROUP_HEADS)], axis=0)

    sub = c // 8

    def part_max(x):
        return jnp.max(x.reshape(sub, 8, x.shape[-1]), axis=0)

    def part_sum(x):
        return jnp.sum(x.reshape(sub, 8, x.shape[-1]), axis=0)

    def score_block(q, k_blk, mask, sc_ref, row0, mpart):
        s = lax.dot_general(k_blk, q, _NT, preferred_element_type=F32)
        s = jnp.where(mask, s, NEG)
        sc_ref[pl.ds(row0, c), :] = s
        return jnp.maximum(mpart, part_max(s))

    def prob_block(vt_blk, sc_ref, row0, m, carry):
        lpart, acc = carry
        p = jnp.exp2(sc_ref[pl.ds(row0, c), :] - m)
        return lpart + part_sum(p), acc + jnp.dot(vt_blk, p.astype(BF16), preferred_element_type=F32)

    mpart0 = jnp.full((8, cols), NEG, F32)
    carry0 = (jnp.zeros((8, cols), F32), jnp.zeros((KV_W, cols), F32))

    def banded(q, k_ref, vt_ref, window, sc_ref, sink=None):
        nblk = _band_blocks(window)
        offs = []
        mpart = mpart0
        for j in range(nblk):
            back = nblk - 1 - j
            kb = i - back
            off = pl.multiple_of(jnp.maximum(kb, 0) * c, c)
            offs.append(off)
            width = jnp.where(kb >= 0, window, 0).astype(jnp.uint32)
            mask = (t_minus_key + back * c).astype(jnp.uint32) < width
            mpart = score_block(q, k_ref[pl.ds(off, c), :], rep(mask), sc_ref, j * c, mpart)
        m = jnp.max(mpart, axis=0, keepdims=True)
        if sink is not None:
            m = jnp.maximum(m, sink)
        carry = carry0
        for j in range(nblk):
            carry = prob_block(vt_ref[:, pl.ds(offs[j], c)], sc_ref, j * c, m, carry)
        l = jnp.sum(carry[0], axis=0, keepdims=True)
        if sink is not None:
            l = l + jnp.exp2(sink - m)
        return carry[1] * (1.0 / jnp.maximum(l, 1e-30))

    def compressed(q):
        s = lax.dot_general(kc_ref[...], q, _NT, preferred_element_type=F32)
        cend = key * CMP_STRIDE + (CMP_BLOCK - 1)
        mk = rep(cend <= tq + i * c)
        s = jnp.where(mk, s, NEG)
        m = jnp.max(s, axis=0, keepdims=True)
        e = jnp.where(mk, jnp.exp2(s - m), 0.0)
        p = e * (1.0 / jnp.maximum(jnp.sum(e, axis=0, keepdims=True), 1e-30))
        o = jnp.dot(vct_ref[...], p.astype(BF16), preferred_element_type=F32)
        psums = []
        for g in range(KV_GROUPS):
            base = g * GROUP_HEADS * c
            psums.append(p[:, base:base + c] + p[:, base + c:base + 2 * c]
                         + p[:, base + 2 * c:base + 3 * c] + p[:, base + 3 * c:base + 4 * c])
        return o, psums

    nsel = pool_ref.shape[0]
    blk = lax.broadcasted_iota(jnp.int32, (nsel, c), 0)
    tcol = lax.broadcasted_iota(jnp.int32, (nsel, c), 1)

    def select_blocks(psum, g):
        hi = psum.astype(BF16)
        lo = (psum - hi.astype(F32)).astype(BF16)
        pool = pool_ref[...]
        imp = jnp.dot(pool, hi, preferred_element_type=F32) + jnp.dot(pool, lo, preferred_element_type=F32)
        t = tcol + i * c
        cur = t >> 6
        forced = (blk == 0) | (blk == cur) | (blk == cur - 1)
        valid = (blk << 6) <= t
        score = jnp.where(forced, FORCE, jnp.where(valid, imp, -1.0))
        rank = jnp.zeros((nsel, c), F32)
        for k in range(nsel):
            sk = score[k:k + 1, :]
            tie = jnp.where(blk > k, 1.0, 0.0)
            rank = rank + jnp.where(sk > score, 1.0, jnp.where(sk == score, tie, 0.0))
        sel_ref[g] = jnp.where(rank < SEL_TOPN, 1.0, 0.0)

    def selected(q, sc_ref):
        nquad = (i + SLC_UNROLL) // SLC_UNROLL

        def block_mask(kb):
            causal = t_minus_key + (i - kb) * c >= 0
            per_group = []
            for g in range(KV_GROUPS):
                first = sel_ref[g, pl.ds(2 * kb, 1), :]
                second = sel_ref[g, pl.ds(2 * kb + 1, 1), :]
                chosen = jnp.where(key < SEL_BLOCK, first, second) > 0.5
                per_group.append(rep(chosen & causal, GROUP_HEADS))
            return jnp.concatenate(per_group, axis=1)

        def phase_a(qd, mpart):
            for u in range(SLC_UNROLL):
                kb = qd * SLC_UNROLL + u
                off = pl.multiple_of(kb * c, c)
                mpart = score_block(q, ks_ref[pl.ds(off, c), :], block_mask(kb), sc_ref, off, mpart)
            return mpart

        m = jnp.max(lax.fori_loop(0, nquad, phase_a, mpart0), axis=0, keepdims=True)

        def phase_b(qd, carry):
            for u in range(SLC_UNROLL):
                off = pl.multiple_of((qd * SLC_UNROLL + u) * c, c)
                carry = prob_block(vst_ref[:, pl.ds(off, c)], sc_ref, off, m, carry)
            return carry

        lpart, acc = lax.fori_loop(0, nquad, phase_b, carry0)
        return acc * (1.0 / jnp.maximum(jnp.sum(lpart, axis=0, keepdims=True), 1e-30))

    def head_block(o, h):
        g = h // GROUP_HEADS
        return o[g * HEAD_DIM:(g + 1) * HEAD_DIM, h * c:(h + 1) * c]

    gs = jax.nn.sigmoid(gt_ref[...])
    q = stack_q(qn_ref)
    o_cmp, psums = compressed(q)
    for g in range(KV_GROUPS):
        select_blocks(psums[g], g)
    o_slc = selected(q, sc_slc)
    o_win = banded(q, kw_ref, vwt_ref, NSA_WINDOW, sc_win)
    heads = [gs[3 * h:3 * h + 1] * head_block(o_cmp, h) + gs[3 * h + 1:3 * h + 2] * head_block(o_slc, h)
             + gs[3 * h + 2:3 * h + 3] * head_block(o_win, h) for h in range(Q_HEADS)]

    head_col = lax.broadcasted_iota(jnp.int32, (1, cols), 1) // c
    sink = jnp.zeros((1, cols), F32)
    for h in range(Q_HEADS):
        sink = jnp.where(head_col == h, sinks_ref[h] * LOG2E, sink)
    o_swa = banded(stack_q(qs_ref), ksw_ref, vswt_ref, SWA_WINDOW, sc_swa, sink)
    heads += [head_block(o_swa, h) for h in range(Q_HEADS)]
    o_ref[...] = jnp.concatenate(heads, axis=0).T.astype(o_ref.dtype)


def _band_blocks(window):
    return (window - 1 + Q_CHUNK - 1) // Q_CHUNK + 1


def attention(sinks, qn, qs, gates_t, kc_c, vc_ct, ks, vst, kw, vwt, ksw, vswt, pool, batch, seq):
    n = batch * seq
    nq = seq // Q_CHUNK
    assert seq // SEL_BLOCK == pool.shape[0] and seq // CMP_STRIDE == LANES
    qspec = lambda w: pl.BlockSpec((Q_CHUNK, w), lambda b, i, *_: (b * nq + i, 0))
    kspec = pl.BlockSpec((seq, KV_W), lambda b, i, *_: (b, 0))
    vtspec = pl.BlockSpec((KV_W, seq), lambda b, i, *_: (0, b))
    grid_spec = pltpu.PrefetchScalarGridSpec(
        num_scalar_prefetch=1,
        grid=(batch, nq),
        in_specs=[qspec(Q_W), qspec(Q_W), pl.BlockSpec((GATE_ROWS, Q_CHUNK), lambda b, i, *_: (0, b * nq + i)),
                  pl.BlockSpec((LANES, KV_W), lambda b, i, *_: (b, 0)), pl.BlockSpec((KV_W, LANES), lambda b, i, *_: (0, b)),
                  kspec, vtspec, kspec, vtspec, kspec, vtspec, _full(pool.shape)],
        out_specs=qspec(2 * Q_W),
        scratch_shapes=[pltpu.VMEM((KV_GROUPS,) + pool.shape[:1] + (Q_CHUNK,), F32),
                        pltpu.VMEM((seq, Q_HEADS * Q_CHUNK), F32),
                        pltpu.VMEM((_band_blocks(NSA_WINDOW) * Q_CHUNK, Q_HEADS * Q_CHUNK), F32),
                        pltpu.VMEM((_band_blocks(SWA_WINDOW) * Q_CHUNK, Q_HEADS * Q_CHUNK), F32)],
    )
    return pl.pallas_call(
        _attn_kernel,
        out_shape=jax.ShapeDtypeStruct((n, 2 * Q_W), BF16),
        grid_spec=grid_spec,
        compiler_params=_cparams(("parallel", "arbitrary")),
        name="attention",
    )(sinks, qn, qs, gates_t, kc_c, vc_ct, ks, vst, kw, vwt, ksw, vswt, pool)


ROW_SLABS = D_MODEL // LANES


def _store_row_slabs(ref, val):
    rows = val.shape[0]
    for s in range(ROW_SLABS):
        ref[pl.ds(s, rows, stride=ROW_SLABS), :] = val[:, s * LANES:(s + 1) * LANES]


def _load_row_slabs(ref, first_row, rows):
    return jnp.concatenate([ref[pl.ds(first_row * ROW_SLABS + s, rows, stride=ROW_SLABS), :]
                            for s in range(ROW_SLABS)], axis=1)


def _layer_norm(y, g, b):
    mu = jnp.mean(y, axis=-1, keepdims=True)
    yc = y - mu
    var = jnp.mean(yc * yc, axis=-1, keepdims=True)
    return yc * lax.rsqrt(var + LN_EPS) * g + b


def _post_attn_kernel(o_ref, x_ref, p_ref, wo_ref, bo_ref, g1_ref, b1_ref,
                      wrh_ref, wrl_ref, br_ref, wpg_ref, bpg_ref, wpp_ref,
                      x1_ref, e_ref, route_ref):
    a = jnp.dot(o_ref[...], wo_ref[...], preferred_element_type=F32) + bo_ref[...]
    x1 = _layer_norm(DN_ALPHA * x_ref[...] + a, g1_ref[...], b1_ref[...])
    _store_row_slabs(x1_ref, x1)
    xh = x1.astype(BF16)
    gate = jax.nn.sigmoid(jnp.dot(xh, wpg_ref[...], preferred_element_type=F32) + bpg_ref[...])
    e_ref[...] = gate * jnp.dot(p_ref[...].astype(BF16), wpp_ref[...], preferred_element_type=F32)
    xl = (x1 - xh.astype(F32)).astype(BF16)
    logits = (lax.dot_general(wrh_ref[...], xh, _NT, preferred_element_type=F32)
              + lax.dot_general(wrh_ref[...], xl, _NT, preferred_element_type=F32)
              + lax.dot_general(wrl_ref[...], xh, _NT, preferred_element_type=F32)) + br_ref[...]
    eidx = lax.broadcasted_iota(jnp.int32, logits.shape, 0).astype(F32)
    vals, idxs = [], []
    for _ in range(TOP_K):
        v = jnp.max(logits, axis=0, keepdims=True)
        ix = jnp.min(jnp.where(logits == v, eidx, float(N_EXPERTS)), axis=0, keepdims=True)
        logits = jnp.where(eidx == ix, -jnp.inf, logits)
        vals.append(v)
        idxs.append(ix)
    exps = [jnp.exp(v - vals[0]) for v in vals]
    den = exps[0] + exps[1] + exps[2] + exps[3]
    route_ref[...] = jnp.concatenate(idxs + [ex / den for ex in exps], axis=0)


def post_attention(o, x2d, p2d, wo, bo, g1, b1, wrh, wrl, br, wpg, bpg, wpp):
    n = x2d.shape[0]
    tm = min(n, 512)
    row = lambda w: pl.BlockSpec((tm, w), lambda i: (i, 0))
    return pl.pallas_call(
        _post_attn_kernel,
        out_shape=(jax.ShapeDtypeStruct((n * ROW_SLABS, LANES), F32), jax.ShapeDtypeStruct((n, D_MODEL), F32),
                   jax.ShapeDtypeStruct((2 * TOP_K, n), F32)),
        grid=(n // tm,),
        in_specs=[row(D_MODEL), row(D_MODEL), row(P_DIM), _full((D_MODEL, D_MODEL)), _full((1, D_MODEL)),
                  _full((1, D_MODEL)), _full((1, D_MODEL)), _full((N_EXPERTS, D_MODEL)), _full((N_EXPERTS, D_MODEL)),
                  _full((N_EXPERTS, 1)), _full((D_MODEL, D_MODEL)), _full((1, D_MODEL)), _full((P_DIM, D_MODEL))],
        out_specs=(pl.BlockSpec((tm * ROW_SLABS, LANES), lambda i: (i, 0)), row(D_MODEL),
                   pl.BlockSpec((2 * TOP_K, tm), lambda i: (0, i))),
        compiler_params=_cparams(("parallel",)),
        name="post_attention",
    )(o, x2d, p2d, wo, bo, g1, b1, wrh, wrl, br, wpg, bpg, wpp)


MOE_TILE = 512
MOE_ROW_GROUPS = 2
MOE_K_CHUNKS = 4
DMA_ISSUE_UNROLL = 8


def _route_tables(expert_idx, tm):
    k, n = expert_idx.shape
    a = k * n
    n_rows = a + (N_EXPERTS + 1) * tm
    flat_e = expert_idx.reshape(a)
    order = jnp.argsort(flat_e, stable=True).astype(jnp.int32)
    cnt = jnp.sum((flat_e[None, :] == jnp.arange(N_EXPERTS, dtype=jnp.int32)[:, None]).astype(jnp.int32), axis=1)
    start = jnp.cumsum(cnt) - cnt
    pcnt = ((cnt + tm - 1) // tm) * tm
    pend = jnp.cumsum(pcnt)
    pstart = pend - pcnt
    nt = n_rows // tm
    q = jnp.arange(n_rows, dtype=jnp.int32)
    tile_start = jnp.arange(nt, dtype=jnp.int32) * tm
    e_of_tile = jnp.sum((tile_start[:, None] >= pend[None, :]).astype(jnp.int32), axis=1)
    e_of_q = jnp.repeat(e_of_tile, tm)
    e_c = jnp.minimum(e_of_q, N_EXPERTS - 1)
    local = q - pstart[e_c]
    valid = (e_of_q < N_EXPERTS) & (local < cnt[e_c])
    r = jnp.clip(start[e_c] + local, 0, a - 1)
    src_tok = jnp.where(valid, order[r] % n, 0).astype(jnp.int32)
    inv = jnp.argsort(order).astype(jnp.int32)
    dest = (pstart[flat_e] + inv - start[flat_e]).astype(jnp.int32)
    tile_e = jnp.minimum(e_of_tile, N_EXPERTS - 1)
    tile_active = (tile_start < pend[-1]).astype(jnp.int32)
    prev = jnp.concatenate([jnp.full((1,), -1, jnp.int32), tile_e[:-1]])
    tile_first = ((tile_e != prev) & (tile_active > 0)).astype(jnp.int32)
    return tile_e, tile_active, tile_first, src_tok.reshape(nt, 1, tm), dest.reshape(k, n)


def _moe_kernel(te_ref, act_ref, first_ref, tok_ref, tok_next_ref,
                x_hbm, w1_ref, b1_ref, w2_ref, b2_ref, y_ref,
                xbuf, w1b, w2b, sem):
    i = pl.program_id(0)
    tm = xbuf.shape[1] // ROW_SLABS
    slot = i % 2

    def row_copy(tref, r, s):
        src = pl.multiple_of(tref[0, 0, r] * ROW_SLABS, ROW_SLABS)
        return pltpu.make_async_copy(x_hbm.at[pl.ds(src, ROW_SLABS)], xbuf.at[s, pl.ds(r * ROW_SLABS, ROW_SLABS)], sem.at[s])

    def wait_slot(s):
        pltpu.make_async_copy(x_hbm.at[pl.ds(0, tm * ROW_SLABS)], xbuf.at[s], sem.at[s]).wait()

    active = act_ref[i] > 0
    requested = jnp.where(i == 0, act_ref[0], act_ref[jnp.maximum(i - 1, 0)]) > 0

    @pl.when((i == 0) & active)
    def _():
        def body(r, carry):
            row_copy(tok_ref, r, 0).start()
            return carry
        lax.fori_loop(0, tm, body, 0, unroll=DMA_ISSUE_UNROLL)

    @pl.when(active)
    def _():
        wait_slot(slot)

        @pl.when(first_ref[i] > 0)
        def _():
            w1b[...] = w1_ref[0].astype(BF16)
            w2b[...] = w2_ref[0].astype(BF16)

        rows = tm // MOE_ROW_GROUPS
        per = rows // MOE_K_CHUNKS
        kw = D_MODEL // MOE_K_CHUNKS
        def first_matmul(r0):
            h = jnp.zeros((rows, 2 * D_FF), F32) + b1_ref[0]
            for kc in range(MOE_K_CHUNKS):
                for r in range(r0 + kc * per, r0 + (kc + 1) * per):
                    row_copy(tok_next_ref, r, 1 - slot).start(priority=r % 2)
                xk = jnp.concatenate([xbuf[slot, pl.ds(r0 * ROW_SLABS + s, rows, stride=ROW_SLABS), :]
                                      for s in range(kc * kw // LANES, (kc + 1) * kw // LANES)], axis=1).astype(BF16)
                h = h + jnp.dot(xk, w1b[kc * kw:(kc + 1) * kw, :], preferred_element_type=F32)
            return h

        def second_matmul(h, r0):
            gate = jnp.minimum(h[:, :D_FF], SWIGLU_LIMIT)
            up = jnp.clip(h[:, D_FF:], -SWIGLU_LIMIT, SWIGLU_LIMIT)
            act = (up + 1.0) * gate * jax.nn.sigmoid(SWIGLU_ALPHA * gate)
            y = jnp.dot(act.astype(BF16), w2b[...], preferred_element_type=F32) + b2_ref[0]
            for s in range(ROW_SLABS):
                y_ref[pl.ds(r0 * ROW_SLABS + s, rows, stride=ROW_SLABS), :] = y[:, s * LANES:(s + 1) * LANES]

        hs = [first_matmul(0)]
        for rg in range(MOE_ROW_GROUPS):
            if rg + 1 < MOE_ROW_GROUPS:
                hs.append(first_matmul((rg + 1) * rows))
            second_matmul(hs[rg], rg * rows)

    @pl.when(jnp.logical_not(active))
    def _():
        y_ref[...] = jnp.zeros_like(y_ref)

    @pl.when(requested & jnp.logical_not(active))
    def _():
        wait_slot(slot)


def routed_experts(x1, tile_e, tile_active, tile_first, src_tok, layer, w_e1, b_e1, w_e2, b_e2):
    nt, _, tm = src_tok.shape
    tokspec = lambda f: pl.BlockSpec((1, 1, tm), f, memory_space=pltpu.SMEM)
    grid_spec = pltpu.PrefetchScalarGridSpec(
        num_scalar_prefetch=3,
        grid=(nt,),
        in_specs=[tokspec(lambda i, *_: (i, 0, 0)),
                  tokspec(lambda i, *_: (jnp.minimum(i + 1, nt - 1), 0, 0)),
                  pl.BlockSpec(memory_space=pl.ANY),
                  pl.BlockSpec((None, 1, D_MODEL, 2 * D_FF), lambda i, te, *_: (layer, te[i], 0, 0)),
                  pl.BlockSpec((1, 1, 2 * D_FF), lambda i, te, *_: (te[i], 0, 0)),
                  pl.BlockSpec((None, 1, D_FF, D_MODEL), lambda i, te, *_: (layer, te[i], 0, 0)),
                  pl.BlockSpec((1, 1, D_MODEL), lambda i, te, *_: (te[i], 0, 0))],
        out_specs=pl.BlockSpec((tm * ROW_SLABS, LANES), lambda i, *_: (i, 0)),
        scratch_shapes=[pltpu.VMEM((2, tm * ROW_SLABS, LANES), F32), pltpu.VMEM((D_MODEL, 2 * D_FF), BF16),
                        pltpu.VMEM((D_FF, D_MODEL), BF16), pltpu.SemaphoreType.DMA((2,))],
    )
    return pl.pallas_call(
        _moe_kernel,
        out_shape=jax.ShapeDtypeStruct((nt * tm * ROW_SLABS, LANES), F32),
        grid_spec=grid_spec,
        compiler_params=_cparams(("arbitrary",)),
        name="routed_experts",
    )(tile_e, tile_active, tile_first, src_tok, src_tok, x1, w_e1, b_e1.reshape(N_EXPERTS, 1, 2 * D_FF),
      w_e2, b_e2.reshape(N_EXPERTS, 1, D_MODEL))


COMBINE_TILE = 128


def _combine_kernel(pos_ref, pos_next_ref, x1_ref, e_ref, gw_ref, g2_ref, b2_ref, ys_hbm, out_ref, buf, sem):
    i = pl.program_id(0)
    nt = pl.num_programs(0)
    tm = out_ref.shape[0]
    n_rows = TOP_K * tm
    slot = i % 2

    def row_copy(pref, r, s):
        src = pl.multiple_of(pref[0, 0, r] * ROW_SLABS, ROW_SLABS)
        return pltpu.make_async_copy(ys_hbm.at[pl.ds(src, ROW_SLABS)], buf.at[s, pl.ds(r * ROW_SLABS, ROW_SLABS)], sem.at[s])

    def wait_slot(s):
        pltpu.make_async_copy(ys_hbm.at[pl.ds(0, n_rows * ROW_SLABS)], buf.at[s], sem.at[s]).wait()

    @pl.when(i == 0)
    def _():
        def body(r, carry):
            row_copy(pos_ref, r, 0).start()
            return carry
        lax.fori_loop(0, n_rows, body, 0, unroll=DMA_ISSUE_UNROLL)

    wait_slot(slot)
    for r in range(n_rows):
        row_copy(pos_next_ref, r, 1 - slot).start(priority=r % 2)

    gw = gw_ref[...]
    m = jnp.zeros((tm, D_MODEL), F32)
    for k in range(TOP_K):
        m = m + gw[:, k:k + 1] * _load_row_slabs(buf.at[slot], k * tm, tm)
    out_ref[...] = _layer_norm(DN_ALPHA * _load_row_slabs(x1_ref, 0, tm) + m + e_ref[...], g2_ref[...], b2_ref[...])

    @pl.when(i == nt - 1)
    def _():
        wait_slot(1 - slot)


def combine_experts(pos, x1, e, gw, g2, b2, ys):
    n = e.shape[0]
    tm = min(n, COMBINE_TILE)
    nt = n // tm
    pos_t = pos.reshape(TOP_K, nt, tm).transpose(1, 0, 2).reshape(nt, 1, TOP_K * tm)
    posspec = lambda f: pl.BlockSpec((1, 1, TOP_K * tm), f, memory_space=pltpu.SMEM)
    row = lambda w: pl.BlockSpec((tm, w), lambda i: (i, 0))
    return pl.pallas_call(
        _combine_kernel,
        out_shape=jax.ShapeDtypeStruct((n, D_MODEL), F32),
        grid=(nt,),
        in_specs=[posspec(lambda i: (i, 0, 0)), posspec(lambda i: (jnp.minimum(i + 1, nt - 1), 0, 0)),
                  pl.BlockSpec((tm * ROW_SLABS, LANES), lambda i: (i, 0)), row(D_MODEL), row(TOP_K),
                  _full((1, D_MODEL)), _full((1, D_MODEL)), pl.BlockSpec(memory_space=pl.ANY)],
        out_specs=row(D_MODEL),
        scratch_shapes=[pltpu.VMEM((2, TOP_K * tm * ROW_SLABS, LANES), F32), pltpu.SemaphoreType.DMA((2,))],
        compiler_params=_cparams(("arbitrary",)),
        name="combine_experts",
    )(pos_t, pos_t, x1, e, gw, g2, b2, ys)


def kernel(x, p, positions, w_in, b_in, ck_pe, w_ck1, w_ck2, cv_pe, w_cv1, w_cv2, sinks, w_o, b_o,
           ln1_g, ln1_b, w_r, b_r, w_e1, b_e1, w_e2, b_e2, w_pg, b_pg, w_pp, ln2_g, ln2_b):
    batch, seq, _ = x.shape
    n = batch * seq
    n_cmp = seq // CMP_STRIDE
    inv = 1.0 / (ROPE_THETA ** (jnp.arange(0, HEAD_DIM, 2, dtype=F32) / HEAD_DIM))
    inv_tiled = jnp.tile(inv, LANES // (HEAD_DIM // 2))[None, :]
    cos, sin = rope_tables(positions.reshape(n), inv_tiled)
    pos_c = jnp.concatenate([positions[:, CMP_BLOCK - 1::CMP_STRIDE], positions[:, -1:]], axis=1)
    cos_c, sin_c = rope_tables(pos_c.reshape(batch * n_cmp), inv_tiled)
    pool = jnp.asarray(_pool_matrix(), BF16)

    x2d = x.reshape(n, D_MODEL)
    for li in range(DEPTH):
        qn, qs, ks, kw, kswa, kc, vc, vst, vwt, vswat, gates_t = input_projection(
            x2d, *_prep_inproj_weights(w_in[li], b_in[li]), cos, sin)
        kweights = _prep_compress_weights(ck_pe[li], w_ck1[li], w_ck2[li], True)
        vweights = _prep_compress_weights(cv_pe[li], w_cv1[li], w_cv2[li], False)
        kc_c, vc_ct = compress_kv(kc.reshape(batch * n_cmp, CMP_STRIDE * KV_W), vc.reshape(batch * n_cmp, CMP_STRIDE * KV_W),
                                  cos_c, sin_c, kweights, vweights)
        o = attention(sinks[li], qn, qs, gates_t, kc_c, vc_ct, ks, vst, kw, vwt, kswa, vswat, pool, batch, seq)

        wr_t = w_r[li].T
        wrh = wr_t.astype(BF16)
        wrl = (wr_t - wrh.astype(F32)).astype(BF16)
        x1, e, route = post_attention(o, x2d, p[li].reshape(n, P_DIM), w_o[li].astype(BF16), b_o[li][None, :],
                                      ln1_g[li][None, :], ln1_b[li][None, :], wrh, wrl, b_r[li][:, None],
                                      w_pg[li].astype(BF16), b_pg[li][None, :], w_pp[li].astype(BF16))
        expert_idx = route[:TOP_K].astype(jnp.int32)
        gw = route[TOP_K:].T
        tile_e, tile_active, tile_first, src_tok, pos = _route_tables(expert_idx, MOE_TILE)
        ys = routed_experts(x1, tile_e, tile_active, tile_first, src_tok, li, w_e1, b_e1[li], w_e2, b_e2[li])
        x2d = combine_experts(pos, x1, e, gw, ln2_g[li][None, :], ln2_b[li][None, :], ys)
    return x2d.reshape(batch, seq, D_MODEL)
```

```python
import numpy as np
import jax
import jax.numpy as jnp
from jax import lax
from jax.experimental import pallas as pl
from jax.experimental.pallas import tpu as pltpu

F32 = jnp.float32
BF16 = jnp.bfloat16

D_MODEL = 1024
DEPTH = 2
HEAD_DIM = 64
Q_HEADS = 8
KV_GROUPS = 2
GROUP_HEADS = Q_HEADS // KV_GROUPS
ROPE_THETA = 10000.0
CMP_STRIDE = 16
CMP_BLOCK = 32
CMP_HIDDEN = 256
SEL_BLOCK = 64
SEL_TOPN = 8
NSA_WINDOW = 512
SWA_WINDOW = 128
N_EXPERTS = 32
TOP_K = 4
D_FF = D_MODEL
SWIGLU_LIMIT = 7.0
SWIGLU_ALPHA = 1.702
P_DIM = 256
DN_ALPHA = (2 * DEPTH) ** 0.25
LN_EPS = 1e-5
NEG = -1e30
FORCE = 1e9
LOG2E = float(np.log2(np.e))

LANES = 128
Q_CHUNK = 128
SLC_UNROLL = 4
KV_W = KV_GROUPS * HEAD_DIM
Q_W = Q_HEADS * HEAD_DIM
VMEM_LIMIT = 56 * 1024 * 1024

_OFF_QN, _OFF_KC, _OFF_VC, _OFF_KS, _OFF_VS, _OFF_KW, _OFF_VW = 0, 512, 640, 768, 896, 1024, 1152
_OFF_GATES, _OFF_QS, _OFF_KSWA, _OFF_VSWA = 1280, 1304, 1816, 1944
N_GATES = Q_HEADS * 3
GATE_ROWS = 32

N_ROPE_TILES = 11
ROPE_W = N_ROPE_TILES * LANES
PROJ_W = 2 * ROPE_W + 2 * LANES
PROJ_T_ROWS = 3 * KV_W + GATE_ROWS


def _paired_head_cols(base):
    cols = []
    for j in range(GROUP_HEADS):
        for half in range(KV_GROUPS):
            h = j + GROUP_HEADS * half
            cols.extend(base + h * HEAD_DIM + d for d in range(HEAD_DIM))
    return cols


def _proj_layout():
    rope_cols = (_paired_head_cols(_OFF_QN) + _paired_head_cols(_OFF_QS)
                 + list(range(_OFF_KS, _OFF_KS + KV_W)) + list(range(_OFF_KW, _OFF_KW + KV_W))
                 + list(range(_OFF_KSWA, _OFF_KSWA + KV_W)))
    rope_cols = np.asarray(rope_cols, np.int32)
    scale = np.ones(ROPE_W, np.float64)
    scale[:2 * Q_W] = HEAD_DIM ** -0.5 * LOG2E
    pos_in_tile = np.arange(ROPE_W) % HEAD_DIM
    first_half = pos_in_tile < HEAD_DIM // 2
    rot_cols = np.where(first_half, rope_cols + HEAD_DIM // 2, rope_cols - HEAD_DIM // 2).astype(np.int32)
    rot_sign = np.where(first_half, -1.0, 1.0)
    plain_cols = np.concatenate([np.arange(o, o + KV_W) for o in (_OFF_KC, _OFF_VC)]).astype(np.int32)
    t_cols = np.concatenate([np.arange(o, o + KV_W) for o in (_OFF_VS, _OFF_VW, _OFF_VSWA)]
                            + [np.arange(_OFF_GATES, _OFF_GATES + N_GATES)]).astype(np.int32)
    return rope_cols, scale.astype(np.float32), rot_cols, (rot_sign * scale).astype(np.float32), plain_cols, t_cols


_ROPE_COLS, _ROPE_SCALE, _ROT_COLS, _ROT_SCALE, _PLAIN_COLS, _T_COLS = _proj_layout()


def _pool_matrix():
    m = np.zeros((32, LANES), np.float32)
    for c in range(LANES - 1):
        m[c // (SEL_BLOCK // CMP_STRIDE), c] = 1.0
    return m


def _cparams(sem, vmem=VMEM_LIMIT):
    return pltpu.CompilerParams(dimension_semantics=sem, vmem_limit_bytes=vmem)


def _full(shape):
    return pl.BlockSpec(shape, lambda *_: (0,) * len(shape))


_NT = (((1,), (1,)), ((), ()))


def _rope_table_kernel(pos_ref, inv_ref, cos_ref, sin_ref):
    ang = pos_ref[...] * inv_ref[...]
    cos_ref[...] = jnp.cos(ang)
    sin_ref[...] = jnp.sin(ang)


def rope_tables(pos, inv_tiled):
    m = pos.shape[0]
    pos_b = jnp.broadcast_to(pos.astype(F32)[:, None], (m, LANES))
    tm = min(m, 1024)
    spec = pl.BlockSpec((tm, LANES), lambda i: (i, 0))
    return pl.pallas_call(
        _rope_table_kernel,
        out_shape=(jax.ShapeDtypeStruct((m, LANES), F32),) * 2,
        grid=(m // tm,),
        in_specs=[spec, _full((1, LANES))],
        out_specs=(spec, spec),
        compiler_params=_cparams(("parallel",)),
        name="rope_tables",
    )(pos_b, inv_tiled)


def _inproj_kernel(x_ref, w_ref, b_ref, wt_ref, bt_ref, cos_ref, sin_ref,
                   qn_ref, qs_ref, ks_ref, kw_ref, kswa_ref, kc_ref, vc_ref,
                   vst_ref, vwt_ref, vswat_ref, gt_ref):
    xb = x_ref[...].astype(BF16)
    cos = cos_ref[...]
    sin = sin_ref[...]

    def proj(lo, hi):
        return jnp.dot(xb, w_ref[:, lo:hi], preferred_element_type=F32) + b_ref[:, lo:hi]

    def roped(tile_lo, n_tiles, out_ref):
        lo, hi = tile_lo * LANES, (tile_lo + n_tiles) * LANES
        h = proj(lo, hi)
        hr = proj(ROPE_W + lo, ROPE_W + hi)
        for t in range(n_tiles):
            sl = slice(t * LANES, (t + 1) * LANES)
            out_ref[:, sl] = (h[:, sl] * cos + hr[:, sl] * sin).astype(out_ref.dtype)

    roped(0, 4, qn_ref)
    roped(4, 4, qs_ref)
    roped(8, 1, ks_ref)
    roped(9, 1, kw_ref)
    roped(10, 1, kswa_ref)
    plain = proj(2 * ROPE_W, PROJ_W)
    kc_ref[...] = plain[:, :LANES].astype(kc_ref.dtype)
    vc_ref[...] = plain[:, LANES:].astype(vc_ref.dtype)
    tr = lax.dot_general(wt_ref[...], xb, _NT, preferred_element_type=F32) + bt_ref[...]
    for t, ref in enumerate((vst_ref, vwt_ref, vswat_ref)):
        ref[...] = tr[t * KV_W:(t + 1) * KV_W].astype(ref.dtype)
    gt_ref[...] = tr[3 * KV_W:]


def input_projection(x2d, w_all, b_all, wt, bt, cos, sin):
    n = x2d.shape[0]
    tm = min(n, 512)
    row = lambda w: pl.BlockSpec((tm, w), lambda i: (i, 0))
    col = lambda h: pl.BlockSpec((h, tm), lambda i: (0, i))
    outs = ([jax.ShapeDtypeStruct((n, Q_W), BF16)] * 2 + [jax.ShapeDtypeStruct((n, KV_W), BF16)] * 5
            + [jax.ShapeDtypeStruct((KV_W, n), BF16)] * 3 + [jax.ShapeDtypeStruct((GATE_ROWS, n), F32)])
    return pl.pallas_call(
        _inproj_kernel,
        out_shape=tuple(outs),
        grid=(n // tm,),
        in_specs=[row(D_MODEL), _full((D_MODEL, PROJ_W)), _full((1, PROJ_W)), _full((PROJ_T_ROWS, D_MODEL)),
                  _full((PROJ_T_ROWS, 1)), row(LANES), row(LANES)],
        out_specs=tuple([row(Q_W)] * 2 + [row(KV_W)] * 5 + [col(KV_W)] * 3 + [col(GATE_ROWS)]),
        compiler_params=_cparams(("parallel",)),
        name="input_projection",
    )(x2d, w_all, b_all, wt, bt, cos, sin)


def _prep_inproj_weights(w_in, b_in):
    w = jnp.concatenate([w_in[:, _ROPE_COLS] * _ROPE_SCALE, w_in[:, _ROT_COLS] * _ROT_SCALE, w_in[:, _PLAIN_COLS]], axis=1)
    b = jnp.concatenate([b_in[_ROPE_COLS] * _ROPE_SCALE, b_in[_ROT_COLS] * _ROT_SCALE, b_in[_PLAIN_COLS]])
    pad = GATE_ROWS - N_GATES
    wt = jnp.concatenate([w_in[:, _T_COLS].T, jnp.zeros((pad, D_MODEL), F32)], axis=0)
    bt = jnp.concatenate([b_in[_T_COLS], jnp.zeros((pad,), F32)])
    return w.astype(BF16), b[None, :], wt.astype(BF16), bt[:, None]


def _gelu_tanh(x):
    return 0.5 * x * (1.0 + jnp.tanh(np.sqrt(2.0 / np.pi) * (x + 0.044715 * (x * x * x))))


def _compress_kernel(kc_ref, vc_ref, cos_ref, sin_ref,
                     kw1a_ref, kw1b_ref, kpe_ref, kw2_ref, kw2r_ref,
                     vw1a_ref, vw1b_ref, vpe_ref, vw2t_ref,
                     ko_ref, vot_ref):
    def hidden(x_ref, w1a_ref, w1b_ref, pe_ref):
        x = x_ref[...]
        ya = jnp.dot(x, w1a_ref[...], preferred_element_type=F32)
        yb = jnp.dot(x, w1b_ref[...], preferred_element_type=F32)
        rows = ya.shape[0]
        h = ya + pltpu.roll(yb, rows - 1, 0)
        pe = pe_ref[...]
        peb = (jnp.dot(pe[:, :CMP_STRIDE * KV_W], w1a_ref[...], preferred_element_type=F32)
               + jnp.dot(pe[:, CMP_STRIDE * KV_W:], w1b_ref[...], preferred_element_type=F32))
        return _gelu_tanh(h + peb[0:1, :]).astype(BF16)

    ak = hidden(kc_ref, kw1a_ref, kw1b_ref, kpe_ref)
    kc = jnp.dot(ak, kw2_ref[...], preferred_element_type=F32)
    kcr = jnp.dot(ak, kw2r_ref[...], preferred_element_type=F32)
    ko_ref[...] = (kc * cos_ref[...] + kcr * sin_ref[...]).astype(ko_ref.dtype)
    av = hidden(vc_ref, vw1a_ref, vw1b_ref, vpe_ref)
    vot_ref[...] = lax.dot_general(vw2t_ref[...], av, _NT, preferred_element_type=F32).astype(vot_ref.dtype)


def _prep_compress_weights(pe, w1, w2, is_key):
    eye = jnp.eye(KV_GROUPS, dtype=F32)
    w1r = w1.reshape(CMP_BLOCK, HEAD_DIM, CMP_HIDDEN)

    def half(wh):
        return jnp.einsum("idh,ge->igdeh", wh, eye).reshape(CMP_STRIDE * KV_W, KV_GROUPS * CMP_HIDDEN).astype(BF16)

    def block_diag(w):
        wb = jnp.einsum("hd,ge->ghed", w, eye)
        return wb.reshape(KV_GROUPS * CMP_HIDDEN, KV_W)

    w1a, w1b = half(w1r[:CMP_STRIDE]), half(w1r[CMP_STRIDE:])
    pe_row = jnp.broadcast_to(pe[:, None, :], (CMP_BLOCK, KV_GROUPS, HEAD_DIM)).reshape(1, CMP_BLOCK * KV_W)
    pe_rows = jnp.broadcast_to(pe_row, (8, CMP_BLOCK * KV_W)).astype(BF16)
    if not is_key:
        return [w1a, w1b, pe_rows, block_diag(w2).T.astype(BF16)]
    half_d = HEAD_DIM // 2
    w2rot = jnp.concatenate([-w2[:, half_d:], w2[:, :half_d]], axis=1)
    return [w1a, w1b, pe_rows, block_diag(w2).astype(BF16), block_diag(w2rot).astype(BF16)]


def compress_kv(kc_chunks, vc_chunks, cos_c, sin_c, kweights, vweights):
    rows = kc_chunks.shape[0]
    tm = min(rows, 512)
    cw = CMP_STRIDE * KV_W
    hw = KV_GROUPS * CMP_HIDDEN
    row = lambda w: pl.BlockSpec((tm, w), lambda i: (i, 0))
    wspecs_k = [_full((cw, hw)), _full((cw, hw)), _full((8, 2 * cw)), _full((hw, KV_W)), _full((hw, KV_W))]
    wspecs_v = wspecs_k[:3] + [_full((KV_W, hw))]
    return pl.pallas_call(
        _compress_kernel,
        out_shape=(jax.ShapeDtypeStruct((rows, KV_W), BF16), jax.ShapeDtypeStruct((KV_W, rows), BF16)),
        grid=(rows // tm,),
        in_specs=[row(cw), row(cw), row(LANES), row(LANES)] + wspecs_k + wspecs_v,
        out_specs=(row(KV_W), pl.BlockSpec((KV_W, tm), lambda i: (0, i))),
        compiler_params=_cparams(("parallel",)),
        name="compress_kv",
    )(kc_chunks, vc_chunks, cos_c, sin_c, *kweights, *vweights)


def _attn_kernel(sinks_ref, qn_ref, qs_ref, gt_ref, kc_ref, vct_ref,
                 ks_ref, vst_ref, kw_ref, vwt_ref, ksw_ref, vswt_ref, pool_ref, o_ref,
                 sel_ref, sc_slc, sc_win, sc_swa):
    i = pl.program_id(1)
    c = Q_CHUNK
    cols = Q_HEADS * c
    key = lax.broadcasted_iota(jnp.int32, (c, c), 0)
    tq = lax.broadcasted_iota(jnp.int32, (c, c), 1)
    t_minus_key = tq - key
    lane = lax.broadcasted_iota(jnp.int32, (c, LANES), 1)
    group_mask = [jnp.where(lane < HEAD_DIM, 1.0, 0.0).astype(BF16), jnp.where(lane < HEAD_DIM, 0.0, 1.0).astype(BF16)]

    def rep(a, times=Q_HEADS):
        return jnp.concatenate([a] * times, axis=1)

    def stack_q(q_ref):
        return jnp.concatenate([q_ref[:, r * LANES:(r + 1) * LANES] * group_mask[g]
                                for g in range(KV_GROUPS) for r in range(GROUP_HEADS)], axis=0)

    sub = c // 8

    def part_max(x):
        return jnp.max(x.reshape(sub, 8, x.shape[-1]), axis=0)

    def part_sum(x):
        return jnp.sum(x.reshape(sub, 8, x.shape[-1]), axis=0)

    def score_rows(q, k_rows, masks, sc_ref, row0, mpart):
        s_all = lax.dot_general(k_rows, q, _NT, preferred_element_type=F32)
        for u, mask in enumerate(masks):
            s = jnp.where(mask, s_all[u * c:(u + 1) * c], NEG)
            sc_ref[pl.ds(row0 + u * c, c), :] = s
            mpart = jnp.maximum(mpart, part_max(s))
        return mpart

    def prob_rows(vt_cols, sc_ref, row0, nblk, m, carry):
        lpart, acc = carry
        probs = []
        for u in range(nblk):
            p = jnp.exp2(sc_ref[pl.ds(row0 + u * c, c), :] - m)
            lpart = lpart + part_sum(p)
            probs.append(p.astype(BF16))
        return lpart, acc + jnp.dot(vt_cols, jnp.concatenate(probs, axis=0), preferred_element_type=F32)

    mpart0 = jnp.full((8, cols), NEG, F32)
    carry0 = (jnp.zeros((8, cols), F32), jnp.zeros((KV_W, cols), F32))

    def banded(q, k_ref, vt_ref, window, sc_ref, sink=None):
        nblk = _band_blocks(window)
        start = pl.multiple_of(jnp.maximum(i - (nblk - 1), 0) * c, c)
        shift = i * c - start
        masks = [rep((t_minus_key + (shift - j * c)).astype(jnp.uint32) < jnp.uint32(window)) for j in range(nblk)]
        mpart = score_rows(q, k_ref[pl.ds(start, nblk * c), :], masks, sc_ref, 0, mpart0)
        m = jnp.max(mpart, axis=0, keepdims=True)
        if sink is not None:
            m = jnp.maximum(m, sink)
        carry = prob_rows(vt_ref[:, pl.ds(start, nblk * c)], sc_ref, 0, nblk, m, carry0)
        l = jnp.sum(carry[0], axis=0, keepdims=True)
        if sink is not None:
            l = l + jnp.exp2(sink - m)
        return carry[1] * (1.0 / jnp.maximum(l, 1e-30))

    def compressed(q):
        s = lax.dot_general(kc_ref[...], q, _NT, preferred_element_type=F32)
        cend = key * CMP_STRIDE + (CMP_BLOCK - 1)
        mk = rep(cend <= tq + i * c)
        s = jnp.where(mk, s, NEG)
        m = jnp.max(s, axis=0, keepdims=True)
        e = jnp.where(mk, jnp.exp2(s - m), 0.0)
        p = e * (1.0 / jnp.maximum(jnp.sum(e, axis=0, keepdims=True), 1e-30))
        o = jnp.dot(vct_ref[...], p.astype(BF16), preferred_element_type=F32)
        psums = []
        for g in range(KV_GROUPS):
            base = g * GROUP_HEADS * c
            psums.append(p[:, base:base + c] + p[:, base + c:base + 2 * c]
                         + p[:, base + 2 * c:base + 3 * c] + p[:, base + 3 * c:base + 4 * c])
        return o, psums

    nsel = pool_ref.shape[0]
    blk = lax.broadcasted_iota(jnp.int32, (nsel, c), 0)
    tcol = lax.broadcasted_iota(jnp.int32, (nsel, c), 1)

    def select_blocks(psum, g):
        hi = psum.astype(BF16)
        lo = (psum - hi.astype(F32)).astype(BF16)
        pool = pool_ref[...]
        imp = jnp.dot(pool, hi, preferred_element_type=F32) + jnp.dot(pool, lo, preferred_element_type=F32)
        t = tcol + i * c
        cur = t >> 6
        forced = (blk == 0) | (blk == cur) | (blk == cur - 1)
        valid = (blk << 6) <= t
        score = jnp.where(forced, FORCE, jnp.where(valid, imp, -1.0))
        rank = jnp.zeros((nsel, c), F32)
        for k in range(nsel):
            sk = score[k:k + 1, :]
            tie = jnp.where(blk > k, 1.0, 0.0)
            rank = rank + jnp.where(sk > score, 1.0, jnp.where(sk == score, tie, 0.0))
        sel_ref[g] = jnp.where(rank < SEL_TOPN, 1.0, 0.0)

    def selected(q, sc_ref):
        nquad = (i + SLC_UNROLL) // SLC_UNROLL

        def block_mask(kb):
            causal = t_minus_key + (i - kb) * c >= 0
            per_group = []
            for g in range(KV_GROUPS):
                first = sel_ref[g, pl.ds(2 * kb, 1), :]
                second = sel_ref[g, pl.ds(2 * kb + 1, 1), :]
                chosen = jnp.where(key < SEL_BLOCK, first, second) > 0.5
                per_group.append(rep(chosen & causal, GROUP_HEADS))
            return jnp.concatenate(per_group, axis=1)

        def phase_a(qd, mpart):
            for u in range(SLC_UNROLL):
                kb = qd * SLC_UNROLL + u
                off = pl.multiple_of(kb * c, c)
                mpart = score_rows(q, ks_ref[pl.ds(off, c), :], [block_mask(kb)], sc_ref, off, mpart)
            return mpart

        m = jnp.max(lax.fori_loop(0, nquad, phase_a, mpart0), axis=0, keepdims=True)

        def phase_b(qd, carry):
            for u in range(SLC_UNROLL):
                off = pl.multiple_of((qd * SLC_UNROLL + u) * c, c)
                carry = prob_rows(vst_ref[:, pl.ds(off, c)], sc_ref, off, 1, m, carry)
            return carry

        lpart, acc = lax.fori_loop(0, nquad, phase_b, carry0)
        return acc * (1.0 / jnp.maximum(jnp.sum(lpart, axis=0, keepdims=True), 1e-30))

    def head_block(o, h):
        g = h // GROUP_HEADS
        return o[g * HEAD_DIM:(g + 1) * HEAD_DIM, h * c:(h + 1) * c]

    gs = jax.nn.sigmoid(gt_ref[...])
    q = stack_q(qn_ref)
    o_cmp, psums = compressed(q)
    for g in range(KV_GROUPS):
        select_blocks(psums[g], g)
    o_slc = selected(q, sc_slc)
    o_win = banded(q, kw_ref, vwt_ref, NSA_WINDOW, sc_win)
    heads = [gs[3 * h:3 * h + 1] * head_block(o_cmp, h) + gs[3 * h + 1:3 * h + 2] * head_block(o_slc, h)
             + gs[3 * h + 2:3 * h + 3] * head_block(o_win, h) for h in range(Q_HEADS)]

    head_col = lax.broadcasted_iota(jnp.int32, (1, cols), 1) // c
    sink = jnp.zeros((1, cols), F32)
    for h in range(Q_HEADS):
        sink = jnp.where(head_col == h, sinks_ref[h] * LOG2E, sink)
    o_swa = banded(stack_q(qs_ref), ksw_ref, vswt_ref, SWA_WINDOW, sc_swa, sink)
    heads += [head_block(o_swa, h) for h in range(Q_HEADS)]
    o_ref[...] = jnp.concatenate(heads, axis=0).T.astype(o_ref.dtype)


def _band_blocks(window):
    return (window - 1 + Q_CHUNK - 1) // Q_CHUNK + 1


def attention(sinks, qn, qs, gates_t, kc_c, vc_ct, ks, vst, kw, vwt, ksw, vswt, pool, batch, seq):
    n = batch * seq
    nq = seq // Q_CHUNK
    assert seq // SEL_BLOCK == pool.shape[0] and seq // CMP_STRIDE == LANES
    qspec = lambda w: pl.BlockSpec((Q_CHUNK, w), lambda b, i, *_: (b * nq + i, 0))
    kspec = pl.BlockSpec((seq, KV_W), lambda b, i, *_: (b, 0))
    vtspec = pl.BlockSpec((KV_W, seq), lambda b, i, *_: (0, b))
    grid_spec = pltpu.PrefetchScalarGridSpec(
        num_scalar_prefetch=1,
        grid=(batch, nq),
        in_specs=[qspec(Q_W), qspec(Q_W), pl.BlockSpec((GATE_ROWS, Q_CHUNK), lambda b, i, *_: (0, b * nq + i)),
                  pl.BlockSpec((LANES, KV_W), lambda b, i, *_: (b, 0)), pl.BlockSpec((KV_W, LANES), lambda b, i, *_: (0, b)),
                  kspec, vtspec, kspec, vtspec, kspec, vtspec, _full(pool.shape)],
        out_specs=qspec(2 * Q_W),
        scratch_shapes=[pltpu.VMEM((KV_GROUPS,) + pool.shape[:1] + (Q_CHUNK,), F32),
                        pltpu.VMEM((seq, Q_HEADS * Q_CHUNK), F32),
                        pltpu.VMEM((_band_blocks(NSA_WINDOW) * Q_CHUNK, Q_HEADS * Q_CHUNK), F32),
                        pltpu.VMEM((_band_blocks(SWA_WINDOW) * Q_CHUNK, Q_HEADS * Q_CHUNK), F32)],
    )
    return pl.pallas_call(
        _attn_kernel,
        out_shape=jax.ShapeDtypeStruct((n, 2 * Q_W), BF16),
        grid_spec=grid_spec,
        compiler_params=_cparams(("parallel", "arbitrary")),
        name="attention",
    )(sinks, qn, qs, gates_t, kc_c, vc_ct, ks, vst, kw, vwt, ksw, vswt, pool)


ROW_SLABS = D_MODEL // LANES


def _store_row_slabs(ref, val):
    rows = val.shape[0]
    for s in range(ROW_SLABS):
        ref[pl.ds(s, rows, stride=ROW_SLABS), :] = val[:, s * LANES:(s + 1) * LANES]


def _load_row_slabs(ref, first_row, rows):
    return jnp.concatenate([ref[pl.ds(first_row * ROW_SLABS + s, rows, stride=ROW_SLABS), :]
                            for s in range(ROW_SLABS)], axis=1)


def _layer_norm(y, g, b):
    mu = jnp.mean(y, axis=-1, keepdims=True)
    yc = y - mu
    var = jnp.mean(yc * yc, axis=-1, keepdims=True)
    return yc * lax.rsqrt(var + LN_EPS) * g + b


def _post_attn_kernel(o_ref, x_ref, p_ref, wo_ref, bo_ref, g1_ref, b1_ref,
                      wrh_ref, wrl_ref, br_ref, wpg_ref, bpg_ref, wpp_ref,
                      x1_ref, e_ref, route_ref):
    a = jnp.dot(o_ref[...], wo_ref[...], preferred_element_type=F32) + bo_ref[...]
    x1 = _layer_norm(DN_ALPHA * x_ref[...] + a, g1_ref[...], b1_ref[...])
    _store_row_slabs(x1_ref, x1)
    xh = x1.astype(BF16)
    gate = jax.nn.sigmoid(jnp.dot(xh, wpg_ref[...], preferred_element_type=F32) + bpg_ref[...])
    e_ref[...] = gate * jnp.dot(p_ref[...].astype(BF16), wpp_ref[...], preferred_element_type=F32)
    xl = (x1 - xh.astype(F32)).astype(BF16)
    logits = (lax.dot_general(wrh_ref[...], xh, _NT, preferred_element_type=F32)
              + lax.dot_general(wrh_ref[...], xl, _NT, preferred_element_type=F32)
              + lax.dot_general(wrl_ref[...], xh, _NT, preferred_element_type=F32)) + br_ref[...]
    eidx = lax.broadcasted_iota(jnp.int32, logits.shape, 0).astype(F32)
    vals, idxs = [], []
    for _ in range(TOP_K):
        v = jnp.max(logits, axis=0, keepdims=True)
        ix = jnp.min(jnp.where(logits == v, eidx, float(N_EXPERTS)), axis=0, keepdims=True)
        logits = jnp.where(eidx == ix, -jnp.inf, logits)
        vals.append(v)
        idxs.append(ix)
    exps = [jnp.exp(v - vals[0]) for v in vals]
    den = exps[0] + exps[1] + exps[2] + exps[3]
    route_ref[...] = jnp.concatenate(idxs + [ex / den for ex in exps], axis=0)


def post_attention(o, x2d, p2d, wo, bo, g1, b1, wrh, wrl, br, wpg, bpg, wpp):
    n = x2d.shape[0]
    tm = min(n, 512)
    row = lambda w: pl.BlockSpec((tm, w), lambda i: (i, 0))
    return pl.pallas_call(
        _post_attn_kernel,
        out_shape=(jax.ShapeDtypeStruct((n * ROW_SLABS, LANES), F32), jax.ShapeDtypeStruct((n, D_MODEL), F32),
                   jax.ShapeDtypeStruct((2 * TOP_K, n), F32)),
        grid=(n // tm,),
        in_specs=[row(D_MODEL), row(D_MODEL), row(P_DIM), _full((D_MODEL, D_MODEL)), _full((1, D_MODEL)),
                  _full((1, D_MODEL)), _full((1, D_MODEL)), _full((N_EXPERTS, D_MODEL)), _full((N_EXPERTS, D_MODEL)),
                  _full((N_EXPERTS, 1)), _full((D_MODEL, D_MODEL)), _full((1, D_MODEL)), _full((P_DIM, D_MODEL))],
        out_specs=(pl.BlockSpec((tm * ROW_SLABS, LANES), lambda i: (i, 0)), row(D_MODEL),
                   pl.BlockSpec((2 * TOP_K, tm), lambda i: (0, i))),
        compiler_params=_cparams(("parallel",)),
        name="post_attention",
    )(o, x2d, p2d, wo, bo, g1, b1, wrh, wrl, br, wpg, bpg, wpp)


MOE_TILE = 256
MOE_K_CHUNKS = 4
DMA_ISSUE_UNROLL = 8


def _route_tables(expert_idx, tm):
    k, n = expert_idx.shape
    a = k * n
    n_rows = a + (N_EXPERTS + 1) * tm
    flat_e = expert_idx.reshape(a)
    order = jnp.argsort(flat_e, stable=True).astype(jnp.int32)
    cnt = jnp.sum((flat_e[:, None] == jnp.arange(N_EXPERTS, dtype=jnp.int32)[None, :]).astype(jnp.int32), axis=0)
    start = jnp.cumsum(cnt) - cnt
    pcnt = ((cnt + tm - 1) // tm) * tm
    pend = jnp.cumsum(pcnt)
    pstart = pend - pcnt
    q = jnp.arange(n_rows, dtype=jnp.int32)
    e_of_q = jnp.sum((q[:, None] >= pend[None, :]).astype(jnp.int32), axis=1)
    e_c = jnp.minimum(e_of_q, N_EXPERTS - 1)
    local = q - pstart[e_c]
    valid = (e_of_q < N_EXPERTS) & (local < cnt[e_c])
    r = jnp.clip(start[e_c] + local, 0, a - 1)
    src_tok = jnp.where(valid, order[r] % n, 0).astype(jnp.int32)
    inv = jnp.argsort(order).astype(jnp.int32)
    dest = (pstart[flat_e] + inv - start[flat_e]).astype(jnp.int32)
    nt = n_rows // tm
    tile_e = e_c[::tm]
    tile_active = (jnp.arange(nt, dtype=jnp.int32) * tm < pend[-1]).astype(jnp.int32)
    prev = jnp.concatenate([jnp.full((1,), -1, jnp.int32), tile_e[:-1]])
    tile_first = ((tile_e != prev) & (tile_active > 0)).astype(jnp.int32)
    return tile_e, tile_active, tile_first, src_tok.reshape(nt, 1, tm), dest.reshape(k, n)


def _moe_kernel(te_ref, act_ref, first_ref, tok_ref, tok_next_ref,
                x_hbm, w1_ref, b1_ref, w2_ref, b2_ref, y_ref,
                xbuf, w1b, w2b, sem):
    i = pl.program_id(0)
    tm = xbuf.shape[1] // ROW_SLABS
    slot = i % 2

    def row_copy(tref, r, s):
        src = pl.multiple_of(tref[0, 0, r] * ROW_SLABS, ROW_SLABS)
        return pltpu.make_async_copy(x_hbm.at[pl.ds(src, ROW_SLABS)], xbuf.at[s, pl.ds(r * ROW_SLABS, ROW_SLABS)], sem.at[s])

    def wait_slot(s):
        pltpu.make_async_copy(x_hbm.at[pl.ds(0, tm * ROW_SLABS)], xbuf.at[s], sem.at[s]).wait()

    active = act_ref[i] > 0
    requested = jnp.where(i == 0, act_ref[0], act_ref[jnp.maximum(i - 1, 0)]) > 0

    @pl.when((i == 0) & active)
    def _():
        def body(r, carry):
            row_copy(tok_ref, r, 0).start()
            return carry
        lax.fori_loop(0, tm, body, 0, unroll=DMA_ISSUE_UNROLL)

    @pl.when(active)
    def _():
        wait_slot(slot)

        @pl.when(first_ref[i] > 0)
        def _():
            w1b[...] = w1_ref[0].astype(BF16)
            w2b[...] = w2_ref[0].astype(BF16)

        per = tm // MOE_K_CHUNKS
        kw = D_MODEL // MOE_K_CHUNKS
        h = jnp.zeros((tm, 2 * D_FF), F32) + b1_ref[0]
        for kc in range(MOE_K_CHUNKS):
            for r in range(kc * per, (kc + 1) * per):
                row_copy(tok_next_ref, r, 1 - slot).start(priority=1)
            xk = jnp.concatenate([xbuf[slot, pl.ds(s, tm, stride=ROW_SLABS), :]
                                  for s in range(kc * kw // LANES, (kc + 1) * kw // LANES)], axis=1).astype(BF16)
            h = h + jnp.dot(xk, w1b[kc * kw:(kc + 1) * kw, :], preferred_element_type=F32)
        gate = jnp.minimum(h[:, :D_FF], SWIGLU_LIMIT)
        up = jnp.clip(h[:, D_FF:], -SWIGLU_LIMIT, SWIGLU_LIMIT)
        act = (up + 1.0) * gate * jax.nn.sigmoid(SWIGLU_ALPHA * gate)
        _store_row_slabs(y_ref, jnp.dot(act.astype(BF16), w2b[...], preferred_element_type=F32) + b2_ref[0])

    @pl.when(jnp.logical_not(active))
    def _():
        y_ref[...] = jnp.zeros_like(y_ref)

    @pl.when(requested & jnp.logical_not(active))
    def _():
        wait_slot(slot)


def routed_experts(x1, tile_e, tile_active, tile_first, src_tok, layer, w_e1, b_e1, w_e2, b_e2):
    nt, _, tm = src_tok.shape
    tokspec = lambda f: pl.BlockSpec((1, 1, tm), f, memory_space=pltpu.SMEM)
    grid_spec = pltpu.PrefetchScalarGridSpec(
        num_scalar_prefetch=3,
        grid=(nt,),
        in_specs=[tokspec(lambda i, *_: (i, 0, 0)),
                  tokspec(lambda i, *_: (jnp.minimum(i + 1, nt - 1), 0, 0)),
                  pl.BlockSpec(memory_space=pl.ANY),
                  pl.BlockSpec((None, 1, D_MODEL, 2 * D_FF), lambda i, te, *_: (layer, te[i], 0, 0)),
                  pl.BlockSpec((1, 1, 2 * D_FF), lambda i, te, *_: (te[i], 0, 0)),
                  pl.BlockSpec((None, 1, D_FF, D_MODEL), lambda i, te, *_: (layer, te[i], 0, 0)),
                  pl.BlockSpec((1, 1, D_MODEL), lambda i, te, *_: (te[i], 0, 0))],
        out_specs=pl.BlockSpec((tm * ROW_SLABS, LANES), lambda i, *_: (i, 0)),
        scratch_shapes=[pltpu.VMEM((2, tm * ROW_SLABS, LANES), F32), pltpu.VMEM((D_MODEL, 2 * D_FF), BF16),
                        pltpu.VMEM((D_FF, D_MODEL), BF16), pltpu.SemaphoreType.DMA((2,))],
    )
    return pl.pallas_call(
        _moe_kernel,
        out_shape=jax.ShapeDtypeStruct((nt * tm * ROW_SLABS, LANES), F32),
        grid_spec=grid_spec,
        compiler_params=_cparams(("arbitrary",)),
        name="routed_experts",
    )(tile_e, tile_active, tile_first, src_tok, src_tok, x1, w_e1, b_e1.reshape(N_EXPERTS, 1, 2 * D_FF),
      w_e2, b_e2.reshape(N_EXPERTS, 1, D_MODEL))


COMBINE_TILE = 128


def _combine_kernel(pos_ref, pos_next_ref, x1_ref, e_ref, gw_ref, g2_ref, b2_ref, ys_hbm, out_ref, buf, sem):
    i = pl.program_id(0)
    nt = pl.num_programs(0)
    tm = out_ref.shape[0]
    n_rows = TOP_K * tm
    slot = i % 2

    def row_copy(pref, r, s):
        src = pl.multiple_of(pref[0, 0, r] * ROW_SLABS, ROW_SLABS)
        return pltpu.make_async_copy(ys_hbm.at[pl.ds(src, ROW_SLABS)], buf.at[s, pl.ds(r * ROW_SLABS, ROW_SLABS)], sem.at[s])

    def wait_slot(s):
        pltpu.make_async_copy(ys_hbm.at[pl.ds(0, n_rows * ROW_SLABS)], buf.at[s], sem.at[s]).wait()

    @pl.when(i == 0)
    def _():
        def body(r, carry):
            row_copy(pos_ref, r, 0).start()
            return carry
        lax.fori_loop(0, n_rows, body, 0, unroll=DMA_ISSUE_UNROLL)

    wait_slot(slot)
    for r in range(n_rows):
        row_copy(pos_next_ref, r, 1 - slot).start(priority=r % 2)

    gw = gw_ref[...]
    m = jnp.zeros((tm, D_MODEL), F32)
    for k in range(TOP_K):
        m = m + gw[:, k:k + 1] * _load_row_slabs(buf.at[slot], k * tm, tm)
    out_ref[...] = _layer_norm(DN_ALPHA * _load_row_slabs(x1_ref, 0, tm) + m + e_ref[...], g2_ref[...], b2_ref[...])

    @pl.when(i == nt - 1)
    def _():
        wait_slot(1 - slot)


def combine_experts(pos, x1, e, gw, g2, b2, ys):
    n = e.shape[0]
    tm = min(n, COMBINE_TILE)
    nt = n // tm
    pos_t = pos.reshape(TOP_K, nt, tm).transpose(1, 0, 2).reshape(nt, 1, TOP_K * tm)
    posspec = lambda f: pl.BlockSpec((1, 1, TOP_K * tm), f, memory_space=pltpu.SMEM)
    row = lambda w: pl.BlockSpec((tm, w), lambda i: (i, 0))
    return pl.pallas_call(
        _combine_kernel,
        out_shape=jax.ShapeDtypeStruct((n, D_MODEL), F32),
        grid=(nt,),
        in_specs=[posspec(lambda i: (i, 0, 0)), posspec(lambda i: (jnp.minimum(i + 1, nt - 1), 0, 0)),
                  pl.BlockSpec((tm * ROW_SLABS, LANES), lambda i: (i, 0)), row(D_MODEL), row(TOP_K),
                  _full((1, D_MODEL)), _full((1, D_MODEL)), pl.BlockSpec(memory_space=pl.ANY)],
        out_specs=row(D_MODEL),
        scratch_shapes=[pltpu.VMEM((2, TOP_K * tm * ROW_SLABS, LANES), F32), pltpu.SemaphoreType.DMA((2,))],
        compiler_params=_cparams(("arbitrary",)),
        name="combine_experts",
    )(pos_t, pos_t, x1, e, gw, g2, b2, ys)


def kernel(x, p, positions, w_in, b_in, ck_pe, w_ck1, w_ck2, cv_pe, w_cv1, w_cv2, sinks, w_o, b_o,
           ln1_g, ln1_b, w_r, b_r, w_e1, b_e1, w_e2, b_e2, w_pg, b_pg, w_pp, ln2_g, ln2_b):
    batch, seq, _ = x.shape
    n = batch * seq
    n_cmp = seq // CMP_STRIDE
    inv = 1.0 / (ROPE_THETA ** (jnp.arange(0, HEAD_DIM, 2, dtype=F32) / HEAD_DIM))
    inv_tiled = jnp.tile(inv, LANES // (HEAD_DIM // 2))[None, :]
    cos, sin = rope_tables(positions.reshape(n), inv_tiled)
    pos_c = jnp.concatenate([positions[:, CMP_BLOCK - 1::CMP_STRIDE], positions[:, -1:]], axis=1)
    cos_c, sin_c = rope_tables(pos_c.reshape(batch * n_cmp), inv_tiled)
    pool = jnp.asarray(_pool_matrix(), BF16)

    x2d = x.reshape(n, D_MODEL)
    for li in range(DEPTH):
        qn, qs, ks, kw, kswa, kc, vc, vst, vwt, vswat, gates_t = input_projection(
            x2d, *_prep_inproj_weights(w_in[li], b_in[li]), cos, sin)
        kweights = _prep_compress_weights(ck_pe[li], w_ck1[li], w_ck2[li], True)
        vweights = _prep_compress_weights(cv_pe[li], w_cv1[li], w_cv2[li], False)
        kc_c, vc_ct = compress_kv(kc.reshape(batch * n_cmp, CMP_STRIDE * KV_W), vc.reshape(batch * n_cmp, CMP_STRIDE * KV_W),
                                  cos_c, sin_c, kweights, vweights)
        o = attention(sinks[li], qn, qs, gates_t, kc_c, vc_ct, ks, vst, kw, vwt, kswa, vswat, pool, batch, seq)

        wr_t = w_r[li].T
        wrh = wr_t.astype(BF16)
        wrl = (wr_t - wrh.astype(F32)).astype(BF16)
        x1, e, route = post_attention(o, x2d, p[li].reshape(n, P_DIM), w_o[li].astype(BF16), b_o[li][None, :],
                                      ln1_g[li][None, :], ln1_b[li][None, :], wrh, wrl, b_r[li][:, None],
                                      w_pg[li].astype(BF16), b_pg[li][None, :], w_pp[li].astype(BF16))
        expert_idx = route[:TOP_K].astype(jnp.int32)
        gw = route[TOP_K:].T
        tile_e, tile_active, tile_first, src_tok, pos = _route_tables(expert_idx, MOE_TILE)
        ys = routed_experts(x1, tile_e, tile_active, tile_first, src_tok, li, w_e1, b_e1[li], w_e2, b_e2[li])
        x2d = combine_experts(pos, x1, e, gw, ln2_g[li][None, :], ln2_b[li][None, :], ys)
    return x2d.reshape(batch, seq, D_MODEL)
```

```python
import numpy as np
import jax
import jax.numpy as jnp
from jax import lax
from jax.experimental import pallas as pl
from jax.experimental.pallas import tpu as pltpu

F32 = jnp.float32
BF16 = jnp.bfloat16

D_MODEL = 1024
DEPTH = 2
HEAD_DIM = 64
Q_HEADS = 8
KV_GROUPS = 2
GROUP_HEADS = Q_HEADS // KV_GROUPS
ROPE_THETA = 10000.0
CMP_STRIDE = 16
CMP_BLOCK = 32
CMP_HIDDEN = 256
SEL_BLOCK = 64
SEL_TOPN = 8
NSA_WINDOW = 512
SWA_WINDOW = 128
N_EXPERTS = 32
TOP_K = 4
D_FF = D_MODEL
SWIGLU_LIMIT = 7.0
SWIGLU_ALPHA = 1.702
P_DIM = 256
DN_ALPHA = (2 * DEPTH) ** 0.25
LN_EPS = 1e-5
NEG = -1e30
FORCE = 1e9
LOG2E = float(np.log2(np.e))

LANES = 128
Q_CHUNK = 128
SLC_UNROLL = 4
KV_W = KV_GROUPS * HEAD_DIM
Q_W = Q_HEADS * HEAD_DIM
VMEM_LIMIT = 56 * 1024 * 1024

_OFF_QN, _OFF_KC, _OFF_VC, _OFF_KS, _OFF_VS, _OFF_KW, _OFF_VW = 0, 512, 640, 768, 896, 1024, 1152
_OFF_GATES, _OFF_QS, _OFF_KSWA, _OFF_VSWA = 1280, 1304, 1816, 1944
N_GATES = Q_HEADS * 3
GATE_ROWS = 32

N_ROPE_TILES = 11
ROPE_W = N_ROPE_TILES * LANES
PROJ_W = ROPE_W + 2 * LANES
PROJ_T_ROWS = 3 * KV_W + GATE_ROWS


def _paired_head_cols(base):
    cols = []
    for j in range(GROUP_HEADS):
        for half in range(KV_GROUPS):
            h = j + GROUP_HEADS * half
            cols.extend(base + h * HEAD_DIM + d for d in range(HEAD_DIM))
    return cols


def _proj_layout():
    rope_cols = (_paired_head_cols(_OFF_QN) + _paired_head_cols(_OFF_QS)
                 + list(range(_OFF_KS, _OFF_KS + KV_W)) + list(range(_OFF_KW, _OFF_KW + KV_W))
                 + list(range(_OFF_KSWA, _OFF_KSWA + KV_W)))
    rope_cols = np.asarray(rope_cols, np.int32)
    scale = np.ones(ROPE_W, np.float64)
    scale[:2 * Q_W] = HEAD_DIM ** -0.5 * LOG2E
    plain_cols = np.concatenate([np.arange(o, o + KV_W) for o in (_OFF_KC, _OFF_VC)]).astype(np.int32)
    t_cols = np.concatenate([np.arange(o, o + KV_W) for o in (_OFF_VS, _OFF_VW, _OFF_VSWA)]
                            + [np.arange(_OFF_GATES, _OFF_GATES + N_GATES)]).astype(np.int32)
    return rope_cols, scale.astype(np.float32), plain_cols, t_cols


_ROPE_COLS, _ROPE_SCALE, _PLAIN_COLS, _T_COLS = _proj_layout()


def _pool_matrix():
    m = np.zeros((32, LANES), np.float32)
    for c in range(LANES - 1):
        m[c // (SEL_BLOCK // CMP_STRIDE), c] = 1.0
    return m


def _cparams(sem, vmem=VMEM_LIMIT):
    return pltpu.CompilerParams(dimension_semantics=sem, vmem_limit_bytes=vmem)


def _full(shape):
    return pl.BlockSpec(shape, lambda *_: (0,) * len(shape))


_NT = (((1,), (1,)), ((), ()))


def _rope_table_kernel(pos_ref, inv_ref, cos_ref, sin_ref):
    ang = pos_ref[...] * inv_ref[...]
    cos_ref[...] = jnp.cos(ang)
    sin_ref[...] = jnp.sin(ang)


def rope_tables(pos, inv_tiled):
    m = pos.shape[0]
    pos_b = jnp.broadcast_to(pos.astype(F32)[:, None], (m, LANES))
    tm = min(m, 1024)
    spec = pl.BlockSpec((tm, LANES), lambda i: (i, 0))
    return pl.pallas_call(
        _rope_table_kernel,
        out_shape=(jax.ShapeDtypeStruct((m, LANES), F32),) * 2,
        grid=(m // tm,),
        in_specs=[spec, _full((1, LANES))],
        out_specs=(spec, spec),
        compiler_params=_cparams(("parallel",)),
        name="rope_tables",
    )(pos_b, inv_tiled)


def _inproj_kernel(x_ref, w_ref, b_ref, wt_ref, bt_ref, cos_ref, sin_ref,
                   qn_ref, qs_ref, ks_ref, kw_ref, kswa_ref, kc_ref, vc_ref,
                   vst_ref, vwt_ref, vswat_ref, gt_ref):
    xb = x_ref[...].astype(BF16)
    cos = cos_ref[...]
    first_half = lax.broadcasted_iota(jnp.int32, cos.shape, 1) % HEAD_DIM < HEAD_DIM // 2
    sin = jnp.where(first_half, -sin_ref[...], sin_ref[...])

    def proj(lo, hi):
        return jnp.dot(xb, w_ref[:, lo:hi], preferred_element_type=F32) + b_ref[:, lo:hi]

    def roped(tile_lo, n_tiles, out_ref):
        h = proj(tile_lo * LANES, (tile_lo + n_tiles) * LANES)
        for t in range(n_tiles):
            ht = h[:, t * LANES:(t + 1) * LANES]
            partner = jnp.where(first_half, pltpu.roll(ht, LANES - HEAD_DIM // 2, 1), pltpu.roll(ht, HEAD_DIM // 2, 1))
            out_ref[:, t * LANES:(t + 1) * LANES] = (ht * cos + partner * sin).astype(out_ref.dtype)

    roped(0, 4, qn_ref)
    roped(4, 4, qs_ref)
    roped(8, 1, ks_ref)
    roped(9, 1, kw_ref)
    roped(10, 1, kswa_ref)
    plain = proj(ROPE_W, PROJ_W)
    kc_ref[...] = plain[:, :LANES].astype(kc_ref.dtype)
    vc_ref[...] = plain[:, LANES:].astype(vc_ref.dtype)
    tr = lax.dot_general(wt_ref[...], xb, _NT, preferred_element_type=F32) + bt_ref[...]
    for t, ref in enumerate((vst_ref, vwt_ref, vswat_ref)):
        ref[...] = tr[t * KV_W:(t + 1) * KV_W].astype(ref.dtype)
    gt_ref[...] = tr[3 * KV_W:]


def input_projection(x2d, w_all, b_all, wt, bt, cos, sin):
    n = x2d.shape[0]
    tm = min(n, 512)
    row = lambda w: pl.BlockSpec((tm, w), lambda i: (i, 0))
    col = lambda h: pl.BlockSpec((h, tm), lambda i: (0, i))
    outs = ([jax.ShapeDtypeStruct((n, Q_W), BF16)] * 2 + [jax.ShapeDtypeStruct((n, KV_W), BF16)] * 5
            + [jax.ShapeDtypeStruct((KV_W, n), BF16)] * 3 + [jax.ShapeDtypeStruct((GATE_ROWS, n), F32)])
    return pl.pallas_call(
        _inproj_kernel,
        out_shape=tuple(outs),
        grid=(n // tm,),
        in_specs=[row(D_MODEL), _full((D_MODEL, PROJ_W)), _full((1, PROJ_W)), _full((PROJ_T_ROWS, D_MODEL)),
                  _full((PROJ_T_ROWS, 1)), row(LANES), row(LANES)],
        out_specs=tuple([row(Q_W)] * 2 + [row(KV_W)] * 5 + [col(KV_W)] * 3 + [col(GATE_ROWS)]),
        compiler_params=_cparams(("parallel",)),
        name="input_projection",
    )(x2d, w_all, b_all, wt, bt, cos, sin)


def _prep_inproj_weights(w_in, b_in):
    w = jnp.concatenate([w_in[:, _ROPE_COLS] * _ROPE_SCALE, w_in[:, _PLAIN_COLS]], axis=1)
    b = jnp.concatenate([b_in[_ROPE_COLS] * _ROPE_SCALE, b_in[_PLAIN_COLS]])
    pad = GATE_ROWS - N_GATES
    wt = jnp.concatenate([w_in[:, _T_COLS].T, jnp.zeros((pad, D_MODEL), F32)], axis=0)
    bt = jnp.concatenate([b_in[_T_COLS], jnp.zeros((pad,), F32)])
    return w.astype(BF16), b[None, :], wt.astype(BF16), bt[:, None]


def _gelu_tanh(x):
    return 0.5 * x * (1.0 + jnp.tanh(np.sqrt(2.0 / np.pi) * (x + 0.044715 * (x * x * x))))


def _compress_kernel(kc_ref, vc_ref, cos_ref, sin_ref,
                     kw1a_ref, kw1b_ref, kpe_ref, kw2_ref, kw2r_ref,
                     vw1a_ref, vw1b_ref, vpe_ref, vw2t_ref,
                     ko_ref, vot_ref):
    def hidden(x_ref, w1a_ref, w1b_ref, pe_ref):
        x = x_ref[...]
        ya = jnp.dot(x, w1a_ref[...], preferred_element_type=F32)
        yb = jnp.dot(x, w1b_ref[...], preferred_element_type=F32)
        rows = ya.shape[0]
        h = ya + pltpu.roll(yb, rows - 1, 0)
        pe = pe_ref[...]
        peb = (jnp.dot(pe[:, :CMP_STRIDE * KV_W], w1a_ref[...], preferred_element_type=F32)
               + jnp.dot(pe[:, CMP_STRIDE * KV_W:], w1b_ref[...], preferred_element_type=F32))
        return _gelu_tanh(h + peb[0:1, :]).astype(BF16)

    ak = hidden(kc_ref, kw1a_ref, kw1b_ref, kpe_ref)
    kc = jnp.dot(ak, kw2_ref[...], preferred_element_type=F32)
    kcr = jnp.dot(ak, kw2r_ref[...], preferred_element_type=F32)
    ko_ref[...] = (kc * cos_ref[...] + kcr * sin_ref[...]).astype(ko_ref.dtype)
    av = hidden(vc_ref, vw1a_ref, vw1b_ref, vpe_ref)
    vot_ref[...] = lax.dot_general(vw2t_ref[...], av, _NT, preferred_element_type=F32).astype(vot_ref.dtype)


def _prep_compress_weights(pe, w1, w2, is_key):
    eye = jnp.eye(KV_GROUPS, dtype=F32)
    w1r = w1.reshape(CMP_BLOCK, HEAD_DIM, CMP_HIDDEN)

    def half(wh):
        return jnp.einsum("idh,ge->igdeh", wh, eye).reshape(CMP_STRIDE * KV_W, KV_GROUPS * CMP_HIDDEN).astype(BF16)

    def block_diag(w):
        wb = jnp.einsum("hd,ge->ghed", w, eye)
        return wb.reshape(KV_GROUPS * CMP_HIDDEN, KV_W)

    w1a, w1b = half(w1r[:CMP_STRIDE]), half(w1r[CMP_STRIDE:])
    pe_row = jnp.broadcast_to(pe[:, None, :], (CMP_BLOCK, KV_GROUPS, HEAD_DIM)).reshape(1, CMP_BLOCK * KV_W)
    pe_rows = jnp.broadcast_to(pe_row, (8, CMP_BLOCK * KV_W)).astype(BF16)
    if not is_key:
        return [w1a, w1b, pe_rows, block_diag(w2).T.astype(BF16)]
    half_d = HEAD_DIM // 2
    w2rot = jnp.concatenate([-w2[:, half_d:], w2[:, :half_d]], axis=1)
    return [w1a, w1b, pe_rows, block_diag(w2).astype(BF16), block_diag(w2rot).astype(BF16)]


def compress_kv(kc_chunks, vc_chunks, cos_c, sin_c, kweights, vweights):
    rows = kc_chunks.shape[0]
    tm = min(rows, 512)
    cw = CMP_STRIDE * KV_W
    hw = KV_GROUPS * CMP_HIDDEN
    row = lambda w: pl.BlockSpec((tm, w), lambda i: (i, 0))
    wspecs_k = [_full((cw, hw)), _full((cw, hw)), _full((8, 2 * cw)), _full((hw, KV_W)), _full((hw, KV_W))]
    wspecs_v = wspecs_k[:3] + [_full((KV_W, hw))]
    return pl.pallas_call(
        _compress_kernel,
        out_shape=(jax.ShapeDtypeStruct((rows, KV_W), BF16), jax.ShapeDtypeStruct((KV_W, rows), BF16)),
        grid=(rows // tm,),
        in_specs=[row(cw), row(cw), row(LANES), row(LANES)] + wspecs_k + wspecs_v,
        out_specs=(row(KV_W), pl.BlockSpec((KV_W, tm), lambda i: (0, i))),
        compiler_params=_cparams(("parallel",)),
        name="compress_kv",
    )(kc_chunks, vc_chunks, cos_c, sin_c, *kweights, *vweights)


def _attn_kernel(sinks_ref, qn_ref, qs_ref, gt_ref, kc_ref, vct_ref,
                 ks_ref, vst_ref, kw_ref, vwt_ref, ksw_ref, vswt_ref, pool_ref, o_ref,
                 sel_ref, sc_slc, sc_win, sc_swa):
    i = pl.program_id(1)
    c = Q_CHUNK
    cols = Q_HEADS * c
    key = lax.broadcasted_iota(jnp.int32, (c, c), 0)
    tq = lax.broadcasted_iota(jnp.int32, (c, c), 1)
    t_minus_key = tq - key
    lane = lax.broadcasted_iota(jnp.int32, (c, LANES), 1)
    group_mask = [jnp.where(lane < HEAD_DIM, 1.0, 0.0).astype(BF16), jnp.where(lane < HEAD_DIM, 0.0, 1.0).astype(BF16)]

    def rep(a, times=Q_HEADS):
        return jnp.concatenate([a] * times, axis=1)

    def stack_q(q_ref):
        return jnp.concatenate([q_ref[:, r * LANES:(r + 1) * LANES] * group_mask[g]
                                for g in range(KV_GROUPS) for r in range(GROUP_HEADS)], axis=0)

    sub = c // 8

    def part_max(x):
        return jnp.max(x.reshape(sub, 8, x.shape[-1]), axis=0)

    def part_sum(x):
        return jnp.sum(x.reshape(sub, 8, x.shape[-1]), axis=0)

    def score_rows(q, k_rows, masks, sc_ref, row0, mpart):
        s_all = lax.dot_general(k_rows, q, _NT, preferred_element_type=F32)
        for u, mask in enumerate(masks):
            s = jnp.where(mask, s_all[u * c:(u + 1) * c], NEG)
            sc_ref[pl.ds(row0 + u * c, c), :] = s
            mpart = jnp.maximum(mpart, part_max(s))
        return mpart

    def prob_rows(vt_cols, sc_ref, row0, nblk, m, carry):
        lpart, acc = carry
        probs = []
        for u in range(nblk):
            p = jnp.exp2(sc_ref[pl.ds(row0 + u * c, c), :] - m)
            lpart = lpart + part_sum(p)
            probs.append(p.astype(BF16))
        return lpart, acc + jnp.dot(vt_cols, jnp.concatenate(probs, axis=0), preferred_element_type=F32)

    mpart0 = jnp.full((8, cols), NEG, F32)
    carry0 = (jnp.zeros((8, cols), F32), jnp.zeros((KV_W, cols), F32))

    def banded(q, k_ref, vt_ref, window, sc_ref, sink=None):
        nblk = _band_blocks(window)
        start = pl.multiple_of(jnp.maximum(i - (nblk - 1), 0) * c, c)
        shift = i * c - start
        masks = [rep((t_minus_key + (shift - j * c)).astype(jnp.uint32) < jnp.uint32(window)) for j in range(nblk)]
        mpart = score_rows(q, k_ref[pl.ds(start, nblk * c), :], masks, sc_ref, 0, mpart0)
        m = jnp.max(mpart, axis=0, keepdims=True)
        if sink is not None:
            m = jnp.maximum(m, sink)
        carry = prob_rows(vt_ref[:, pl.ds(start, nblk * c)], sc_ref, 0, nblk, m, carry0)
        l = jnp.sum(carry[0], axis=0, keepdims=True)
        if sink is not None:
            l = l + jnp.exp2(sink - m)
        return carry[1] * (1.0 / jnp.maximum(l, 1e-30))

    def compressed(q):
        s = lax.dot_general(kc_ref[...], q, _NT, preferred_element_type=F32)
        cend = key * CMP_STRIDE + (CMP_BLOCK - 1)
        mk = rep(cend <= tq + i * c)
        s = jnp.where(mk, s, NEG)
        m = jnp.max(s, axis=0, keepdims=True)
        e = jnp.where(mk, jnp.exp2(s - m), 0.0)
        p = e * (1.0 / jnp.maximum(jnp.sum(e, axis=0, keepdims=True), 1e-30))
        o = jnp.dot(vct_ref[...], p.astype(BF16), preferred_element_type=F32)
        psums = []
        for g in range(KV_GROUPS):
            base = g * GROUP_HEADS * c
            psums.append(p[:, base:base + c] + p[:, base + c:base + 2 * c]
                         + p[:, base + 2 * c:base + 3 * c] + p[:, base + 3 * c:base + 4 * c])
        return o, psums

    nsel = pool_ref.shape[0]
    blk = lax.broadcasted_iota(jnp.int32, (nsel, c), 0)
    tcol = lax.broadcasted_iota(jnp.int32, (nsel, c), 1)

    def select_blocks(psum, g):
        hi = psum.astype(BF16)
        lo = (psum - hi.astype(F32)).astype(BF16)
        pool = pool_ref[...]
        imp = jnp.dot(pool, hi, preferred_element_type=F32) + jnp.dot(pool, lo, preferred_element_type=F32)
        t = tcol + i * c
        cur = t >> 6
        forced = (blk == 0) | (blk == cur) | (blk == cur - 1)
        valid = (blk << 6) <= t
        score = jnp.where(forced, FORCE, jnp.where(valid, imp, -1.0))
        rank = jnp.zeros((nsel, c), F32)
        for k in range(nsel):
            sk = score[k:k + 1, :]
            tie = jnp.where(blk > k, 1.0, 0.0)
            rank = rank + jnp.where(sk > score, 1.0, jnp.where(sk == score, tie, 0.0))
        sel_ref[g] = jnp.where(rank < SEL_TOPN, 1.0, 0.0)

    def selected(q, sc_ref):
        nquad = (i + SLC_UNROLL) // SLC_UNROLL

        def block_mask(kb):
            causal = t_minus_key + (i - kb) * c >= 0
            per_group = []
            for g in range(KV_GROUPS):
                first = sel_ref[g, pl.ds(2 * kb, 1), :]
                second = sel_ref[g, pl.ds(2 * kb + 1, 1), :]
                chosen = jnp.where(key < SEL_BLOCK, first, second) > 0.5
                per_group.append(rep(chosen & causal, GROUP_HEADS))
            return jnp.concatenate(per_group, axis=1)

        def phase_a(qd, mpart):
            for u in range(SLC_UNROLL):
                kb = qd * SLC_UNROLL + u
                off = pl.multiple_of(kb * c, c)
                mpart = score_rows(q, ks_ref[pl.ds(off, c), :], [block_mask(kb)], sc_ref, off, mpart)
            return mpart

        m = jnp.max(lax.fori_loop(0, nquad, phase_a, mpart0), axis=0, keepdims=True)

        def phase_b(qd, carry):
            for u in range(SLC_UNROLL):
                off = pl.multiple_of((qd * SLC_UNROLL + u) * c, c)
                carry = prob_rows(vst_ref[:, pl.ds(off, c)], sc_ref, off, 1, m, carry)
            return carry

        lpart, acc = lax.fori_loop(0, nquad, phase_b, carry0)
        return acc * (1.0 / jnp.maximum(jnp.sum(lpart, axis=0, keepdims=True), 1e-30))

    def head_block(o, h):
        g = h // GROUP_HEADS
        return o[g * HEAD_DIM:(g + 1) * HEAD_DIM, h * c:(h + 1) * c]

    gs = jax.nn.sigmoid(gt_ref[...])
    q = stack_q(qn_ref)
    o_cmp, psums = compressed(q)
    for g in range(KV_GROUPS):
        select_blocks(psums[g], g)
    o_slc = selected(q, sc_slc)
    o_win = banded(q, kw_ref, vwt_ref, NSA_WINDOW, sc_win)
    heads = [gs[3 * h:3 * h + 1] * head_block(o_cmp, h) + gs[3 * h + 1:3 * h + 2] * head_block(o_slc, h)
             + gs[3 * h + 2:3 * h + 3] * head_block(o_win, h) for h in range(Q_HEADS)]

    head_col = lax.broadcasted_iota(jnp.int32, (1, cols), 1) // c
    sink = jnp.zeros((1, cols), F32)
    for h in range(Q_HEADS):
        sink = jnp.where(head_col == h, sinks_ref[h] * LOG2E, sink)
    o_swa = banded(stack_q(qs_ref), ksw_ref, vswt_ref, SWA_WINDOW, sc_swa, sink)
    heads += [head_block(o_swa, h) for h in range(Q_HEADS)]
    o_ref[...] = jnp.concatenate(heads, axis=0).T.astype(o_ref.dtype)


def _band_blocks(window):
    return (window - 1 + Q_CHUNK - 1) // Q_CHUNK + 1


def attention(sinks, qn, qs, gates_t, kc_c, vc_ct, ks, vst, kw, vwt, ksw, vswt, pool, batch, seq):
    n = batch * seq
    nq = seq // Q_CHUNK
    assert seq // SEL_BLOCK == pool.shape[0] and seq // CMP_STRIDE == LANES
    qspec = lambda w: pl.BlockSpec((Q_CHUNK, w), lambda b, i, *_: (b * nq + i, 0))
    kspec = pl.BlockSpec((seq, KV_W), lambda b, i, *_: (b, 0))
    vtspec = pl.BlockSpec((KV_W, seq), lambda b, i, *_: (0, b))
    grid_spec = pltpu.PrefetchScalarGridSpec(
        num_scalar_prefetch=1,
        grid=(batch, nq),
        in_specs=[qspec(Q_W), qspec(Q_W), pl.BlockSpec((GATE_ROWS, Q_CHUNK), lambda b, i, *_: (0, b * nq + i)),
                  pl.BlockSpec((LANES, KV_W), lambda b, i, *_: (b, 0)), pl.BlockSpec((KV_W, LANES), lambda b, i, *_: (0, b)),
                  kspec, vtspec, kspec, vtspec, kspec, vtspec, _full(pool.shape)],
        out_specs=qspec(2 * Q_W),
        scratch_shapes=[pltpu.VMEM((KV_GROUPS,) + pool.shape[:1] + (Q_CHUNK,), F32),
                        pltpu.VMEM((seq, Q_HEADS * Q_CHUNK), F32),
                        pltpu.VMEM((_band_blocks(NSA_WINDOW) * Q_CHUNK, Q_HEADS * Q_CHUNK), F32),
                        pltpu.VMEM((_band_blocks(SWA_WINDOW) * Q_CHUNK, Q_HEADS * Q_CHUNK), F32)],
    )
    return pl.pallas_call(
        _attn_kernel,
        out_shape=jax.ShapeDtypeStruct((n, 2 * Q_W), BF16),
        grid_spec=grid_spec,
        compiler_params=_cparams(("parallel", "arbitrary")),
        name="attention",
    )(sinks, qn, qs, gates_t, kc_c, vc_ct, ks, vst, kw, vwt, ksw, vswt, pool)


ROW_SLABS = D_MODEL // LANES


def _store_row_slabs(ref, val):
    rows = val.shape[0]
    for s in range(ROW_SLABS):
        ref[pl.ds(s, rows, stride=ROW_SLABS), :] = val[:, s * LANES:(s + 1) * LANES]


def _load_row_slabs(ref, first_row, rows):
    return jnp.concatenate([ref[pl.ds(first_row * ROW_SLABS + s, rows, stride=ROW_SLABS), :]
                            for s in range(ROW_SLABS)], axis=1)


def _layer_norm(y, g, b):
    mu = jnp.mean(y, axis=-1, keepdims=True)
    yc = y - mu
    var = jnp.mean(yc * yc, axis=-1, keepdims=True)
    return yc * lax.rsqrt(var + LN_EPS) * g + b


def _post_attn_kernel(o_ref, x_ref, p_ref, wo_ref, bo_ref, g1_ref, b1_ref,
                      wrh_ref, wrl_ref, br_ref, wpg_ref, bpg_ref, wpp_ref,
                      x1_ref, e_ref, route_ref):
    a = jnp.dot(o_ref[...], wo_ref[...], preferred_element_type=F32) + bo_ref[...]
    x1 = _layer_norm(DN_ALPHA * x_ref[...] + a, g1_ref[...], b1_ref[...])
    _store_row_slabs(x1_ref, x1)
    xh = x1.astype(BF16)
    gate = jax.nn.sigmoid(jnp.dot(xh, wpg_ref[...], preferred_element_type=F32) + bpg_ref[...])
    e_ref[...] = gate * jnp.dot(p_ref[...].astype(BF16), wpp_ref[...], preferred_element_type=F32)
    xl = (x1 - xh.astype(F32)).astype(BF16)
    logits = (lax.dot_general(wrh_ref[...], xh, _NT, preferred_element_type=F32)
              + lax.dot_general(wrh_ref[...], xl, _NT, preferred_element_type=F32)
              + lax.dot_general(wrl_ref[...], xh, _NT, preferred_element_type=F32)) + br_ref[...]
    eidx = lax.broadcasted_iota(jnp.int32, logits.shape, 0).astype(F32)
    vals, idxs = [], []
    for _ in range(TOP_K):
        v = jnp.max(logits, axis=0, keepdims=True)
        ix = jnp.min(jnp.where(logits == v, eidx, float(N_EXPERTS)), axis=0, keepdims=True)
        logits = jnp.where(eidx == ix, -jnp.inf, logits)
        vals.append(v)
        idxs.append(ix)
    exps = [jnp.exp(v - vals[0]) for v in vals]
    den = exps[0] + exps[1] + exps[2] + exps[3]
    route_ref[...] = jnp.concatenate(idxs + [ex / den for ex in exps], axis=0)


def post_attention(o, x2d, p2d, wo, bo, g1, b1, wrh, wrl, br, wpg, bpg, wpp):
    n = x2d.shape[0]
    tm = min(n, 512)
    row = lambda w: pl.BlockSpec((tm, w), lambda i: (i, 0))
    return pl.pallas_call(
        _post_attn_kernel,
        out_shape=(jax.ShapeDtypeStruct((n * ROW_SLABS, LANES), F32), jax.ShapeDtypeStruct((n, D_MODEL), F32),
                   jax.ShapeDtypeStruct((2 * TOP_K, n), F32)),
        grid=(n // tm,),
        in_specs=[row(D_MODEL), row(D_MODEL), row(P_DIM), _full((D_MODEL, D_MODEL)), _full((1, D_MODEL)),
                  _full((1, D_MODEL)), _full((1, D_MODEL)), _full((N_EXPERTS, D_MODEL)), _full((N_EXPERTS, D_MODEL)),
                  _full((N_EXPERTS, 1)), _full((D_MODEL, D_MODEL)), _full((1, D_MODEL)), _full((P_DIM, D_MODEL))],
        out_specs=(pl.BlockSpec((tm * ROW_SLABS, LANES), lambda i: (i, 0)), row(D_MODEL),
                   pl.BlockSpec((2 * TOP_K, tm), lambda i: (0, i))),
        compiler_params=_cparams(("parallel",)),
        name="post_attention",
    )(o, x2d, p2d, wo, bo, g1, b1, wrh, wrl, br, wpg, bpg, wpp)


MOE_TILE = 256
MOE_K_CHUNKS = 4
DMA_ISSUE_UNROLL = 8


def _route_tables(expert_idx, tm):
    k, n = expert_idx.shape
    a = k * n
    n_rows = a + (N_EXPERTS + 1) * tm
    flat_e = expert_idx.reshape(a)
    order = jnp.argsort(flat_e, stable=True).astype(jnp.int32)
    cnt = jnp.sum((flat_e[:, None] == jnp.arange(N_EXPERTS, dtype=jnp.int32)[None, :]).astype(jnp.int32), axis=0)
    start = jnp.cumsum(cnt) - cnt
    pcnt = ((cnt + tm - 1) // tm) * tm
    pend = jnp.cumsum(pcnt)
    pstart = pend - pcnt
    q = jnp.arange(n_rows, dtype=jnp.int32)
    e_of_q = jnp.sum((q[:, None] >= pend[None, :]).astype(jnp.int32), axis=1)
    e_c = jnp.minimum(e_of_q, N_EXPERTS - 1)
    local = q - pstart[e_c]
    valid = (e_of_q < N_EXPERTS) & (local < cnt[e_c])
    r = jnp.clip(start[e_c] + local, 0, a - 1)
    src_tok = jnp.where(valid, order[r] % n, 0).astype(jnp.int32)
    inv = jnp.argsort(order).astype(jnp.int32)
    dest = (pstart[flat_e] + inv - start[flat_e]).astype(jnp.int32)
    nt = n_rows // tm
    tile_e = e_c[::tm]
    tile_active = (jnp.arange(nt, dtype=jnp.int32) * tm < pend[-1]).astype(jnp.int32)
    prev = jnp.concatenate([jnp.full((1,), -1, jnp.int32), tile_e[:-1]])
    tile_first = ((tile_e != prev) & (tile_active > 0)).astype(jnp.int32)
    return tile_e, tile_active, tile_first, src_tok.reshape(nt, 1, tm), dest.reshape(k, n)


def _moe_kernel(te_ref, act_ref, first_ref, tok_ref, tok_next_ref,
                x_hbm, w1_ref, b1_ref, w2_ref, b2_ref, y_ref,
                xbuf, w1b, w2b, sem):
    i = pl.program_id(0)
    tm = xbuf.shape[1] // ROW_SLABS
    slot = i % 2

    def row_copy(tref, r, s):
        src = pl.multiple_of(tref[0, 0, r] * ROW_SLABS, ROW_SLABS)
        return pltpu.make_async_copy(x_hbm.at[pl.ds(src, ROW_SLABS)], xbuf.at[s, pl.ds(r * ROW_SLABS, ROW_SLABS)], sem.at[s])

    def wait_slot(s):
        pltpu.make_async_copy(x_hbm.at[pl.ds(0, tm * ROW_SLABS)], xbuf.at[s], sem.at[s]).wait()

    active = act_ref[i] > 0
    requested = jnp.where(i == 0, act_ref[0], act_ref[jnp.maximum(i - 1, 0)]) > 0

    @pl.when((i == 0) & active)
    def _():
        def body(r, carry):
            row_copy(tok_ref, r, 0).start()
            return carry
        lax.fori_loop(0, tm, body, 0, unroll=DMA_ISSUE_UNROLL)

    @pl.when(active)
    def _():
        wait_slot(slot)

        @pl.when(first_ref[i] > 0)
        def _():
            w1b[...] = w1_ref[0].astype(BF16)
            w2b[...] = w2_ref[0].astype(BF16)

        per = tm // MOE_K_CHUNKS
        kw = D_MODEL // MOE_K_CHUNKS
        h = jnp.zeros((tm, 2 * D_FF), F32) + b1_ref[0]
        for kc in range(MOE_K_CHUNKS):
            for r in range(kc * per, (kc + 1) * per):
                row_copy(tok_next_ref, r, 1 - slot).start(priority=1)
            xk = jnp.concatenate([xbuf[slot, pl.ds(s, tm, stride=ROW_SLABS), :]
                                  for s in range(kc * kw // LANES, (kc + 1) * kw // LANES)], axis=1).astype(BF16)
            h = h + jnp.dot(xk, w1b[kc * kw:(kc + 1) * kw, :], preferred_element_type=F32)
        gate = jnp.minimum(h[:, :D_FF], SWIGLU_LIMIT)
        up = jnp.clip(h[:, D_FF:], -SWIGLU_LIMIT, SWIGLU_LIMIT)
        act = (up + 1.0) * gate * jax.nn.sigmoid(SWIGLU_ALPHA * gate)
        _store_row_slabs(y_ref, jnp.dot(act.astype(BF16), w2b[...], preferred_element_type=F32) + b2_ref[0])

    @pl.when(jnp.logical_not(active))
    def _():
        y_ref[...] = jnp.zeros_like(y_ref)

    @pl.when(requested & jnp.logical_not(active))
    def _():
        wait_slot(slot)


def routed_experts(x1, tile_e, tile_active, tile_first, src_tok, layer, w_e1, b_e1, w_e2, b_e2):
    nt, _, tm = src_tok.shape
    tokspec = lambda f: pl.BlockSpec((1, 1, tm), f, memory_space=pltpu.SMEM)
    grid_spec = pltpu.PrefetchScalarGridSpec(
        num_scalar_prefetch=3,
        grid=(nt,),
        in_specs=[tokspec(lambda i, *_: (i, 0, 0)),
                  tokspec(lambda i, *_: (jnp.minimum(i + 1, nt - 1), 0, 0)),
                  pl.BlockSpec(memory_space=pl.ANY),
                  pl.BlockSpec((None, 1, D_MODEL, 2 * D_FF), lambda i, te, *_: (layer, te[i], 0, 0)),
                  pl.BlockSpec((1, 1, 2 * D_FF), lambda i, te, *_: (te[i], 0, 0)),
                  pl.BlockSpec((None, 1, D_FF, D_MODEL), lambda i, te, *_: (layer, te[i], 0, 0)),
                  pl.BlockSpec((1, 1, D_MODEL), lambda i, te, *_: (te[i], 0, 0))],
        out_specs=pl.BlockSpec((tm * ROW_SLABS, LANES), lambda i, *_: (i, 0)),
        scratch_shapes=[pltpu.VMEM((2, tm * ROW_SLABS, LANES), F32), pltpu.VMEM((D_MODEL, 2 * D_FF), BF16),
                        pltpu.VMEM((D_FF, D_MODEL), BF16), pltpu.SemaphoreType.DMA((2,))],
    )
    return pl.pallas_call(
        _moe_kernel,
        out_shape=jax.ShapeDtypeStruct((nt * tm * ROW_SLABS, LANES), F32),
        grid_spec=grid_spec,
        compiler_params=_cparams(("arbitrary",)),
        name="routed_experts",
    )(tile_e, tile_active, tile_first, src_tok, src_tok, x1, w_e1, b_e1.reshape(N_EXPERTS, 1, 2 * D_FF),
      w_e2, b_e2.reshape(N_EXPERTS, 1, D_MODEL))


COMBINE_TILE = 128


def _combine_kernel(pos_ref, pos_next_ref, x1_ref, e_ref, gw_ref, g2_ref, b2_ref, ys_hbm, out_ref, buf, sem):
    i = pl.program_id(0)
    nt = pl.num_programs(0)
    tm = out_ref.shape[0]
    n_rows = TOP_K * tm
    slot = i % 2

    def row_copy(pref, r, s):
        src = pl.multiple_of(pref[0, 0, r] * ROW_SLABS, ROW_SLABS)
        return pltpu.make_async_copy(ys_hbm.at[pl.ds(src, ROW_SLABS)], buf.at[s, pl.ds(r * ROW_SLABS, ROW_SLABS)], sem.at[s])

    def wait_slot(s):
        pltpu.make_async_copy(ys_hbm.at[pl.ds(0, n_rows * ROW_SLABS)], buf.at[s], sem.at[s]).wait()

    @pl.when(i == 0)
    def _():
        def body(r, carry):
            row_copy(pos_ref, r, 0).start()
            return carry
        lax.fori_loop(0, n_rows, body, 0, unroll=DMA_ISSUE_UNROLL)

    wait_slot(slot)
    for r in range(n_rows):
        row_copy(pos_next_ref, r, 1 - slot).start(priority=r % 2)

    gw = gw_ref[...]
    m = jnp.zeros((tm, D_MODEL), F32)
    for k in range(TOP_K):
        m = m + gw[:, k:k + 1] * _load_row_slabs(buf.at[slot], k * tm, tm)
    out_ref[...] = _layer_norm(DN_ALPHA * _load_row_slabs(x1_ref, 0, tm) + m + e_ref[...], g2_ref[...], b2_ref[...])

    @pl.when(i == nt - 1)
    def _():
        wait_slot(1 - slot)


def combine_experts(pos, x1, e, gw, g2, b2, ys):
    n = e.shape[0]
    tm = min(n, COMBINE_TILE)
    nt = n // tm
    pos_t = pos.reshape(TOP_K, nt, tm).transpose(1, 0, 2).reshape(nt, 1, TOP_K * tm)
    posspec = lambda f: pl.BlockSpec((1, 1, TOP_K * tm), f, memory_space=pltpu.SMEM)
    row = lambda w: pl.BlockSpec((tm, w), lambda i: (i, 0))
    return pl.pallas_call(
        _combine_kernel,
        out_shape=jax.ShapeDtypeStruct((n, D_MODEL), F32),
        grid=(nt,),
        in_specs=[posspec(lambda i: (i, 0, 0)), posspec(lambda i: (jnp.minimum(i + 1, nt - 1), 0, 0)),
                  pl.BlockSpec((tm * ROW_SLABS, LANES), lambda i: (i, 0)), row(D_MODEL), row(TOP_K),
                  _full((1, D_MODEL)), _full((1, D_MODEL)), pl.BlockSpec(memory_space=pl.ANY)],
        out_specs=row(D_MODEL),
        scratch_shapes=[pltpu.VMEM((2, TOP_K * tm * ROW_SLABS, LANES), F32), pltpu.SemaphoreType.DMA((2,))],
        compiler_params=_cparams(("arbitrary",)),
        name="combine_experts",
    )(pos_t, pos_t, x1, e, gw, g2, b2, ys)


def kernel(x, p, positions, w_in, b_in, ck_pe, w_ck1, w_ck2, cv_pe, w_cv1, w_cv2, sinks, w_o, b_o,
           ln1_g, ln1_b, w_r, b_r, w_e1, b_e1, w_e2, b_e2, w_pg, b_pg, w_pp, ln2_g, ln2_b):
    batch, seq, _ = x.shape
    n = batch * seq
    n_cmp = seq // CMP_STRIDE
    inv = 1.0 / (ROPE_THETA ** (jnp.arange(0, HEAD_DIM, 2, dtype=F32) / HEAD_DIM))
    inv_tiled = jnp.tile(inv, LANES // (HEAD_DIM // 2))[None, :]
    cos, sin = rope_tables(positions.reshape(n), inv_tiled)
    pos_c = jnp.concatenate([positions[:, CMP_BLOCK - 1::CMP_STRIDE], positions[:, -1:]], axis=1)
    cos_c, sin_c = rope_tables(pos_c.reshape(batch * n_cmp), inv_tiled)
    pool = jnp.asarray(_pool_matrix(), BF16)

    x2d = x.reshape(n, D_MODEL)
    for li in range(DEPTH):
        qn, qs, ks, kw, kswa, kc, vc, vst, vwt, vswat, gates_t = input_projection(
            x2d, *_prep_inproj_weights(w_in[li], b_in[li]), cos, sin)
        kweights = _prep_compress_weights(ck_pe[li], w_ck1[li], w_ck2[li], True)
        vweights = _prep_compress_weights(cv_pe[li], w_cv1[li], w_cv2[li], False)
        kc_c, vc_ct = compress_kv(kc.reshape(batch * n_cmp, CMP_STRIDE * KV_W), vc.reshape(batch * n_cmp, CMP_STRIDE * KV_W),
                                  cos_c, sin_c, kweights, vweights)
        o = attention(sinks[li], qn, qs, gates_t, kc_c, vc_ct, ks, vst, kw, vwt, kswa, vswat, pool, batch, seq)

        wr_t = w_r[li].T
        wrh = wr_t.astype(BF16)
        wrl = (wr_t - wrh.astype(F32)).astype(BF16)
        x1, e, route = post_attention(o, x2d, p[li].reshape(n, P_DIM), w_o[li].astype(BF16), b_o[li][None, :],
                                      ln1_g[li][None, :], ln1_b[li][None, :], wrh, wrl, b_r[li][:, None],
                                      w_pg[li].astype(BF16), b_pg[li][None, :], w_pp[li].astype(BF16))
        expert_idx = route[:TOP_K].astype(jnp.int32)
        gw = route[TOP_K:].T
        tile_e, tile_active, tile_first, src_tok, pos = _route_tables(expert_idx, MOE_TILE)
        ys = routed_experts(x1, tile_e, tile_active, tile_first, src_tok, li, w_e1, b_e1[li], w_e2, b_e2[li])
        x2d = combine_experts(pos, x1, e, gw, ln2_g[li][None, :], ln2_b[li][None, :], ys)
    return x2d.reshape(batch, seq, D_MODEL)
```

```python
import numpy as np
import jax
import jax.numpy as jnp
from jax import lax
from jax.experimental import pallas as pl
from jax.experimental.pallas import tpu as pltpu

F32 = jnp.float32
BF16 = jnp.bfloat16

D_MODEL = 1024
DEPTH = 2
HEAD_DIM = 64
Q_HEADS = 8
KV_GROUPS = 2
GROUP_HEADS = Q_HEADS // KV_GROUPS
ROPE_THETA = 10000.0
CMP_STRIDE = 16
CMP_BLOCK = 32
CMP_HIDDEN = 256
SEL_BLOCK = 64
SEL_TOPN = 8
NSA_WINDOW = 512
SWA_WINDOW = 128
N_EXPERTS = 32
TOP_K = 4
D_FF = D_MODEL
SWIGLU_LIMIT = 7.0
SWIGLU_ALPHA = 1.702
P_DIM = 256
DN_ALPHA = (2 * DEPTH) ** 0.25
LN_EPS = 1e-5
NEG = -1e30
FORCE = 1e9
LOG2E = float(np.log2(np.e))

LANES = 128
Q_CHUNK = 128
SLC_UNROLL = 4
KV_W = KV_GROUPS * HEAD_DIM
Q_W = Q_HEADS * HEAD_DIM
VMEM_LIMIT = 56 * 1024 * 1024

_OFF_QN, _OFF_KC, _OFF_VC, _OFF_KS, _OFF_VS, _OFF_KW, _OFF_VW = 0, 512, 640, 768, 896, 1024, 1152
_OFF_GATES, _OFF_QS, _OFF_KSWA, _OFF_VSWA = 1280, 1304, 1816, 1944
N_GATES = Q_HEADS * 3
GATE_ROWS = 32

N_ROPE_TILES = 11
ROPE_W = N_ROPE_TILES * LANES
PROJ_W = ROPE_W + 2 * LANES
PROJ_T_ROWS = 3 * KV_W + GATE_ROWS


def _paired_head_cols(base):
    cols = []
    for j in range(GROUP_HEADS):
        for half in range(KV_GROUPS):
            h = j + GROUP_HEADS * half
            cols.extend(base + h * HEAD_DIM + d for d in range(HEAD_DIM))
    return cols


def _proj_layout():
    rope_cols = (_paired_head_cols(_OFF_QN) + _paired_head_cols(_OFF_QS)
                 + list(range(_OFF_KS, _OFF_KS + KV_W)) + list(range(_OFF_KW, _OFF_KW + KV_W))
                 + list(range(_OFF_KSWA, _OFF_KSWA + KV_W)))
    rope_cols = np.asarray(rope_cols, np.int32)
    scale = np.ones(ROPE_W, np.float64)
    scale[:2 * Q_W] = HEAD_DIM ** -0.5 * LOG2E
    plain_cols = np.concatenate([np.arange(o, o + KV_W) for o in (_OFF_KC, _OFF_VC)]).astype(np.int32)
    t_cols = np.concatenate([np.arange(o, o + KV_W) for o in (_OFF_VS, _OFF_VW, _OFF_VSWA)]
                            + [np.arange(_OFF_GATES, _OFF_GATES + N_GATES)]).astype(np.int32)
    return rope_cols, scale.astype(np.float32), plain_cols, t_cols


_ROPE_COLS, _ROPE_SCALE, _PLAIN_COLS, _T_COLS = _proj_layout()


def _pool_matrix():
    m = np.zeros((32, LANES), np.float32)
    for c in range(LANES - 1):
        m[c // (SEL_BLOCK // CMP_STRIDE), c] = 1.0
    return m


def _cparams(sem, vmem=VMEM_LIMIT):
    return pltpu.CompilerParams(dimension_semantics=sem, vmem_limit_bytes=vmem)


def _full(shape):
    return pl.BlockSpec(shape, lambda *_: (0,) * len(shape))


_NT = (((1,), (1,)), ((), ()))


def _rope_table_kernel(pos_ref, inv_ref, cos_ref, sin_ref):
    ang = pos_ref[...] * inv_ref[...]
    cos_ref[...] = jnp.cos(ang)
    sin_ref[...] = jnp.sin(ang)


def rope_tables(pos, inv_tiled):
    m = pos.shape[0]
    pos_b = jnp.broadcast_to(pos.astype(F32)[:, None], (m, LANES))
    tm = min(m, 1024)
    spec = pl.BlockSpec((tm, LANES), lambda i: (i, 0))
    return pl.pallas_call(
        _rope_table_kernel,
        out_shape=(jax.ShapeDtypeStruct((m, LANES), F32),) * 2,
        grid=(m // tm,),
        in_specs=[spec, _full((1, LANES))],
        out_specs=(spec, spec),
        compiler_params=_cparams(("parallel",)),
        name="rope_tables",
    )(pos_b, inv_tiled)


def _inproj_kernel(x_ref, w_ref, b_ref, wt_ref, bt_ref, cos_ref, sin_ref,
                   qn_ref, qs_ref, ks_ref, kw_ref, kswa_ref, kc_ref, vc_ref,
                   vst_ref, vwt_ref, vswat_ref, gt_ref):
    xb = x_ref[...].astype(BF16)
    cos = cos_ref[...]
    first_half = lax.broadcasted_iota(jnp.int32, cos.shape, 1) % HEAD_DIM < HEAD_DIM // 2
    sin = jnp.where(first_half, -sin_ref[...], sin_ref[...])

    def proj(lo, hi):
        return jnp.dot(xb, w_ref[:, lo:hi], preferred_element_type=F32) + b_ref[:, lo:hi]

    def roped(tile_lo, n_tiles, out_ref):
        h = proj(tile_lo * LANES, (tile_lo + n_tiles) * LANES)
        for t in range(n_tiles):
            ht = h[:, t * LANES:(t + 1) * LANES]
            partner = jnp.where(first_half, pltpu.roll(ht, LANES - HEAD_DIM // 2, 1), pltpu.roll(ht, HEAD_DIM // 2, 1))
            out_ref[:, t * LANES:(t + 1) * LANES] = (ht * cos + partner * sin).astype(out_ref.dtype)

    roped(0, 4, qn_ref)
    roped(4, 4, qs_ref)
    roped(8, 1, ks_ref)
    roped(9, 1, kw_ref)
    roped(10, 1, kswa_ref)
    plain = proj(ROPE_W, PROJ_W)
    kc_ref[...] = plain[:, :LANES].astype(kc_ref.dtype)
    vc_ref[...] = plain[:, LANES:].astype(vc_ref.dtype)
    tr = lax.dot_general(wt_ref[...], xb, _NT, preferred_element_type=F32) + bt_ref[...]
    for t, ref in enumerate((vst_ref, vwt_ref, vswat_ref)):
        ref[...] = tr[t * KV_W:(t + 1) * KV_W].astype(ref.dtype)
    gt_ref[...] = tr[3 * KV_W:]


def input_projection(x2d, w_all, b_all, wt, bt, cos, sin):
    n = x2d.shape[0]
    tm = min(n, 512)
    row = lambda w: pl.BlockSpec((tm, w), lambda i: (i, 0))
    col = lambda h: pl.BlockSpec((h, tm), lambda i: (0, i))
    outs = ([jax.ShapeDtypeStruct((n, Q_W), BF16)] * 2 + [jax.ShapeDtypeStruct((n, KV_W), BF16)] * 5
            + [jax.ShapeDtypeStruct((KV_W, n), BF16)] * 3 + [jax.ShapeDtypeStruct((GATE_ROWS, n), F32)])
    return pl.pallas_call(
        _inproj_kernel,
        out_shape=tuple(outs),
        grid=(n // tm,),
        in_specs=[row(D_MODEL), _full((D_MODEL, PROJ_W)), _full((1, PROJ_W)), _full((PROJ_T_ROWS, D_MODEL)),
                  _full((PROJ_T_ROWS, 1)), row(LANES), row(LANES)],
        out_specs=tuple([row(Q_W)] * 2 + [row(KV_W)] * 5 + [col(KV_W)] * 3 + [col(GATE_ROWS)]),
        compiler_params=_cparams(("parallel",)),
        name="input_projection",
    )(x2d, w_all, b_all, wt, bt, cos, sin)


def _prep_inproj_weights(w_in, b_in):
    w = jnp.concatenate([w_in[:, _ROPE_COLS] * _ROPE_SCALE, w_in[:, _PLAIN_COLS]], axis=1)
    b = jnp.concatenate([b_in[_ROPE_COLS] * _ROPE_SCALE, b_in[_PLAIN_COLS]])
    pad = GATE_ROWS - N_GATES
    wt = jnp.concatenate([w_in[:, _T_COLS].T, jnp.zeros((pad, D_MODEL), F32)], axis=0)
    bt = jnp.concatenate([b_in[_T_COLS], jnp.zeros((pad,), F32)])
    return w.astype(BF16), b[None, :], wt.astype(BF16), bt[:, None]


def _gelu_tanh(x):
    return 0.5 * x * (1.0 + jnp.tanh(np.sqrt(2.0 / np.pi) * (x + 0.044715 * (x * x * x))))


def _compress_kernel(kc_ref, vc_ref, cos_ref, sin_ref,
                     kw1a_ref, kw1b_ref, kpe_ref, kw2_ref, kw2r_ref,
                     vw1a_ref, vw1b_ref, vpe_ref, vw2t_ref,
                     ko_ref, vot_ref):
    def hidden(x_ref, w1a_ref, w1b_ref, pe_ref):
        x = x_ref[...]
        ya = jnp.dot(x, w1a_ref[...], preferred_element_type=F32)
        yb = jnp.dot(x, w1b_ref[...], preferred_element_type=F32)
        rows = ya.shape[0]
        h = ya + pltpu.roll(yb, rows - 1, 0)
        pe = pe_ref[...]
        peb = (jnp.dot(pe[:, :CMP_STRIDE * KV_W], w1a_ref[...], preferred_element_type=F32)
               + jnp.dot(pe[:, CMP_STRIDE * KV_W:], w1b_ref[...], preferred_element_type=F32))
        return _gelu_tanh(h + peb[0:1, :]).astype(BF16)

    ak = hidden(kc_ref, kw1a_ref, kw1b_ref, kpe_ref)
    kc = jnp.dot(ak, kw2_ref[...], preferred_element_type=F32)
    kcr = jnp.dot(ak, kw2r_ref[...], preferred_element_type=F32)
    ko_ref[...] = (kc * cos_ref[...] + kcr * sin_ref[...]).astype(ko_ref.dtype)
    av = hidden(vc_ref, vw1a_ref, vw1b_ref, vpe_ref)
    vot_ref[...] = lax.dot_general(vw2t_ref[...], av, _NT, preferred_element_type=F32).astype(vot_ref.dtype)


def _prep_compress_weights(pe, w1, w2, is_key):
    eye = jnp.eye(KV_GROUPS, dtype=F32)
    w1r = w1.reshape(CMP_BLOCK, HEAD_DIM, CMP_HIDDEN)

    def half(wh):
        return jnp.einsum("idh,ge->igdeh", wh, eye).reshape(CMP_STRIDE * KV_W, KV_GROUPS * CMP_HIDDEN).astype(BF16)

    def block_diag(w):
        wb = jnp.einsum("hd,ge->ghed", w, eye)
        return wb.reshape(KV_GROUPS * CMP_HIDDEN, KV_W)

    w1a, w1b = half(w1r[:CMP_STRIDE]), half(w1r[CMP_STRIDE:])
    pe_row = jnp.broadcast_to(pe[:, None, :], (CMP_BLOCK, KV_GROUPS, HEAD_DIM)).reshape(1, CMP_BLOCK * KV_W)
    pe_rows = jnp.broadcast_to(pe_row, (8, CMP_BLOCK * KV_W)).astype(BF16)
    if not is_key:
        return [w1a, w1b, pe_rows, block_diag(w2).T.astype(BF16)]
    half_d = HEAD_DIM // 2
    w2rot = jnp.concatenate([-w2[:, half_d:], w2[:, :half_d]], axis=1)
    return [w1a, w1b, pe_rows, block_diag(w2).astype(BF16), block_diag(w2rot).astype(BF16)]


def compress_kv(kc_chunks, vc_chunks, cos_c, sin_c, kweights, vweights):
    rows = kc_chunks.shape[0]
    tm = min(rows, 512)
    cw = CMP_STRIDE * KV_W
    hw = KV_GROUPS * CMP_HIDDEN
    row = lambda w: pl.BlockSpec((tm, w), lambda i: (i, 0))
    wspecs_k = [_full((cw, hw)), _full((cw, hw)), _full((8, 2 * cw)), _full((hw, KV_W)), _full((hw, KV_W))]
    wspecs_v = wspecs_k[:3] + [_full((KV_W, hw))]
    return pl.pallas_call(
        _compress_kernel,
        out_shape=(jax.ShapeDtypeStruct((rows, KV_W), BF16), jax.ShapeDtypeStruct((KV_W, rows), BF16)),
        grid=(rows // tm,),
        in_specs=[row(cw), row(cw), row(LANES), row(LANES)] + wspecs_k + wspecs_v,
        out_specs=(row(KV_W), pl.BlockSpec((KV_W, tm), lambda i: (0, i))),
        compiler_params=_cparams(("parallel",)),
        name="compress_kv",
    )(kc_chunks, vc_chunks, cos_c, sin_c, *kweights, *vweights)


def _attn_kernel(sinks_ref, qn_ref, qs_ref, gt_ref, kc_ref, vct_ref,
                 ks_ref, vst_ref, kw_ref, vwt_ref, ksw_ref, vswt_ref, pool_ref, o_ref,
                 sel_ref, sc_slc, sc_win, sc_swa):
    i = pl.program_id(1)
    c = Q_CHUNK
    cols = Q_HEADS * c
    key = lax.broadcasted_iota(jnp.int32, (c, c), 0)
    tq = lax.broadcasted_iota(jnp.int32, (c, c), 1)
    t_minus_key = tq - key
    lane = lax.broadcasted_iota(jnp.int32, (c, LANES), 1)
    group_mask = [jnp.where(lane < HEAD_DIM, 1.0, 0.0).astype(BF16), jnp.where(lane < HEAD_DIM, 0.0, 1.0).astype(BF16)]

    def rep(a, times=Q_HEADS):
        return jnp.concatenate([a] * times, axis=1)

    def stack_q(q_ref):
        return jnp.concatenate([q_ref[:, r * LANES:(r + 1) * LANES] * group_mask[g]
                                for g in range(KV_GROUPS) for r in range(GROUP_HEADS)], axis=0)

    sub = c // 8

    def part_max(x):
        return jnp.max(x.reshape(sub, 8, x.shape[-1]), axis=0)

    def part_sum(x):
        return jnp.sum(x.reshape(sub, 8, x.shape[-1]), axis=0)

    def score_rows(q, k_rows, masks, sc_ref, row0, mpart):
        s_all = lax.dot_general(k_rows, q, _NT, preferred_element_type=F32)
        for u, mask in enumerate(masks):
            s = jnp.where(mask, s_all[u * c:(u + 1) * c], NEG)
            sc_ref[pl.ds(row0 + u * c, c), :] = s
            mpart = jnp.maximum(mpart, part_max(s))
        return mpart

    def group_pv(vt, probs):
        half = cols // KV_GROUPS
        return jnp.concatenate([jnp.dot(vt[g * HEAD_DIM:(g + 1) * HEAD_DIM], probs[:, g * half:(g + 1) * half],
                                        preferred_element_type=F32) for g in range(KV_GROUPS)], axis=1)

    def prob_rows(vt_cols, sc_ref, row0, nblk, m, carry):
        lpart, acc = carry
        probs = []
        for u in range(nblk):
            p = jnp.exp2(sc_ref[pl.ds(row0 + u * c, c), :] - m)
            lpart = lpart + part_sum(p)
            probs.append(p.astype(BF16))
        return lpart, acc + group_pv(vt_cols, jnp.concatenate(probs, axis=0))

    mpart0 = jnp.full((8, cols), NEG, F32)
    carry0 = (jnp.zeros((8, cols), F32), jnp.zeros((HEAD_DIM, cols), F32))

    def banded(q, k_ref, vt_ref, window, sc_ref, sink=None):
        nblk = _band_blocks(window)
        start = pl.multiple_of(jnp.maximum(i - (nblk - 1), 0) * c, c)
        shift = i * c - start
        masks = [rep((t_minus_key + (shift - j * c)).astype(jnp.uint32) < jnp.uint32(window)) for j in range(nblk)]
        mpart = score_rows(q, k_ref[pl.ds(start, nblk * c), :], masks, sc_ref, 0, mpart0)
        m = jnp.max(mpart, axis=0, keepdims=True)
        if sink is not None:
            m = jnp.maximum(m, sink)
        carry = prob_rows(vt_ref[:, pl.ds(start, nblk * c)], sc_ref, 0, nblk, m, carry0)
        l = jnp.sum(carry[0], axis=0, keepdims=True)
        if sink is not None:
            l = l + jnp.exp2(sink - m)
        return carry[1] * (1.0 / jnp.maximum(l, 1e-30))

    def compressed(q):
        s = lax.dot_general(kc_ref[...], q, _NT, preferred_element_type=F32)
        cend = key * CMP_STRIDE + (CMP_BLOCK - 1)
        mk = rep(cend <= tq + i * c)
        s = jnp.where(mk, s, NEG)
        m = jnp.max(s, axis=0, keepdims=True)
        e = jnp.where(mk, jnp.exp2(s - m), 0.0)
        p = e * (1.0 / jnp.maximum(jnp.sum(e, axis=0, keepdims=True), 1e-30))
        o = group_pv(vct_ref[...], p.astype(BF16))
        psums = []
        for g in range(KV_GROUPS):
            base = g * GROUP_HEADS * c
            psums.append(p[:, base:base + c] + p[:, base + c:base + 2 * c]
                         + p[:, base + 2 * c:base + 3 * c] + p[:, base + 3 * c:base + 4 * c])
        return o, psums

    nsel = pool_ref.shape[0]
    blk = lax.broadcasted_iota(jnp.int32, (nsel, c), 0)
    tcol = lax.broadcasted_iota(jnp.int32, (nsel, c), 1)

    def select_blocks(psum, g):
        hi = psum.astype(BF16)
        lo = (psum - hi.astype(F32)).astype(BF16)
        pool = pool_ref[...]
        imp = jnp.dot(pool, hi, preferred_element_type=F32) + jnp.dot(pool, lo, preferred_element_type=F32)
        t = tcol + i * c
        cur = t >> 6
        forced = (blk == 0) | (blk == cur) | (blk == cur - 1)
        valid = (blk << 6) <= t
        score = jnp.where(forced, FORCE, jnp.where(valid, imp, -1.0))
        rank = jnp.zeros((nsel, c), F32)
        for k in range(nsel):
            sk = score[k:k + 1, :]
            tie = jnp.where(blk > k, 1.0, 0.0)
            rank = rank + jnp.where(sk > score, 1.0, jnp.where(sk == score, tie, 0.0))
        sel_ref[g] = jnp.where(rank < SEL_TOPN, 1.0, 0.0)

    def selected(q, sc_ref):
        nquad = (i + SLC_UNROLL) // SLC_UNROLL

        def block_mask(kb):
            causal = t_minus_key + (i - kb) * c >= 0
            per_group = []
            for g in range(KV_GROUPS):
                first = sel_ref[g, pl.ds(2 * kb, 1), :]
                second = sel_ref[g, pl.ds(2 * kb + 1, 1), :]
                chosen = jnp.where(key < SEL_BLOCK, first, second) > 0.5
                per_group.append(rep(chosen & causal, GROUP_HEADS))
            return jnp.concatenate(per_group, axis=1)

        def phase_a(qd, mpart):
            for u in range(SLC_UNROLL):
                kb = qd * SLC_UNROLL + u
                off = pl.multiple_of(kb * c, c)
                mpart = score_rows(q, ks_ref[pl.ds(off, c), :], [block_mask(kb)], sc_ref, off, mpart)
            return mpart

        m = jnp.max(lax.fori_loop(0, nquad, phase_a, mpart0), axis=0, keepdims=True)

        def phase_b(qd, carry):
            for u in range(SLC_UNROLL):
                off = pl.multiple_of((qd * SLC_UNROLL + u) * c, c)
                carry = prob_rows(vst_ref[:, pl.ds(off, c)], sc_ref, off, 1, m, carry)
            return carry

        lpart, acc = lax.fori_loop(0, nquad, phase_b, carry0)
        return acc * (1.0 / jnp.maximum(jnp.sum(lpart, axis=0, keepdims=True), 1e-30))

    def head_block(o, h):
        return o[:, h * c:(h + 1) * c]

    gs = jax.nn.sigmoid(gt_ref[...])
    q = stack_q(qn_ref)
    o_cmp, psums = compressed(q)
    for g in range(KV_GROUPS):
        select_blocks(psums[g], g)
    o_slc = selected(q, sc_slc)
    o_win = banded(q, kw_ref, vwt_ref, NSA_WINDOW, sc_win)
    heads = [gs[3 * h:3 * h + 1] * head_block(o_cmp, h) + gs[3 * h + 1:3 * h + 2] * head_block(o_slc, h)
             + gs[3 * h + 2:3 * h + 3] * head_block(o_win, h) for h in range(Q_HEADS)]

    head_col = lax.broadcasted_iota(jnp.int32, (1, cols), 1) // c
    sink = jnp.zeros((1, cols), F32)
    for h in range(Q_HEADS):
        sink = jnp.where(head_col == h, sinks_ref[h] * LOG2E, sink)
    o_swa = banded(stack_q(qs_ref), ksw_ref, vswt_ref, SWA_WINDOW, sc_swa, sink)
    heads += [head_block(o_swa, h) for h in range(Q_HEADS)]
    o_ref[...] = jnp.concatenate(heads, axis=0).T.astype(o_ref.dtype)


def _band_blocks(window):
    return (window - 1 + Q_CHUNK - 1) // Q_CHUNK + 1


def attention(sinks, qn, qs, gates_t, kc_c, vc_ct, ks, vst, kw, vwt, ksw, vswt, pool, batch, seq):
    n = batch * seq
    nq = seq // Q_CHUNK
    assert seq // SEL_BLOCK == pool.shape[0] and seq // CMP_STRIDE == LANES
    qspec = lambda w: pl.BlockSpec((Q_CHUNK, w), lambda b, i, *_: (b * nq + i, 0))
    kspec = pl.BlockSpec((seq, KV_W), lambda b, i, *_: (b, 0))
    vtspec = pl.BlockSpec((KV_W, seq), lambda b, i, *_: (0, b))
    grid_spec = pltpu.PrefetchScalarGridSpec(
        num_scalar_prefetch=1,
        grid=(batch, nq),
        in_specs=[qspec(Q_W), qspec(Q_W), pl.BlockSpec((GATE_ROWS, Q_CHUNK), lambda b, i, *_: (0, b * nq + i)),
                  pl.BlockSpec((LANES, KV_W), lambda b, i, *_: (b, 0)), pl.BlockSpec((KV_W, LANES), lambda b, i, *_: (0, b)),
                  kspec, vtspec, kspec, vtspec, kspec, vtspec, _full(pool.shape)],
        out_specs=qspec(2 * Q_W),
        scratch_shapes=[pltpu.VMEM((KV_GROUPS,) + pool.shape[:1] + (Q_CHUNK,), F32),
                        pltpu.VMEM((seq, Q_HEADS * Q_CHUNK), F32),
                        pltpu.VMEM((_band_blocks(NSA_WINDOW) * Q_CHUNK, Q_HEADS * Q_CHUNK), F32),
                        pltpu.VMEM((_band_blocks(SWA_WINDOW) * Q_CHUNK, Q_HEADS * Q_CHUNK), F32)],
    )
    return pl.pallas_call(
        _attn_kernel,
        out_shape=jax.ShapeDtypeStruct((n, 2 * Q_W), BF16),
        grid_spec=grid_spec,
        compiler_params=_cparams(("parallel", "arbitrary")),
        name="attention",
    )(sinks, qn, qs, gates_t, kc_c, vc_ct, ks, vst, kw, vwt, ksw, vswt, pool)


ROW_SLABS = D_MODEL // LANES


def _store_row_slabs(ref, val):
    rows = val.shape[0]
    for s in range(ROW_SLABS):
        ref[pl.ds(s, rows, stride=ROW_SLABS), :] = val[:, s * LANES:(s + 1) * LANES]


def _load_row_slabs(ref, first_row, rows):
    return jnp.concatenate([ref[pl.ds(first_row * ROW_SLABS + s, rows, stride=ROW_SLABS), :]
                            for s in range(ROW_SLABS)], axis=1)


def _layer_norm(y, g, b):
    mu = jnp.mean(y, axis=-1, keepdims=True)
    yc = y - mu
    var = jnp.mean(yc * yc, axis=-1, keepdims=True)
    return yc * lax.rsqrt(var + LN_EPS) * g + b


def _post_attn_kernel(o_ref, x_ref, p_ref, wo_ref, bo_ref, g1_ref, b1_ref,
                      wrh_ref, wrl_ref, br_ref, wpg_ref, bpg_ref, wpp_ref,
                      x1_ref, e_ref, route_ref):
    a = jnp.dot(o_ref[...], wo_ref[...], preferred_element_type=F32) + bo_ref[...]
    x1 = _layer_norm(DN_ALPHA * x_ref[...] + a, g1_ref[...], b1_ref[...])
    _store_row_slabs(x1_ref, x1)
    xh = x1.astype(BF16)
    gate = jax.nn.sigmoid(jnp.dot(xh, wpg_ref[...], preferred_element_type=F32) + bpg_ref[...])
    e_ref[...] = gate * jnp.dot(p_ref[...].astype(BF16), wpp_ref[...], preferred_element_type=F32)
    xl = (x1 - xh.astype(F32)).astype(BF16)
    logits = (lax.dot_general(wrh_ref[...], xh, _NT, preferred_element_type=F32)
              + lax.dot_general(wrh_ref[...], xl, _NT, preferred_element_type=F32)
              + lax.dot_general(wrl_ref[...], xh, _NT, preferred_element_type=F32)) + br_ref[...]
    eidx = lax.broadcasted_iota(jnp.int32, logits.shape, 0).astype(F32)
    vals, idxs = [], []
    for _ in range(TOP_K):
        v = jnp.max(logits, axis=0, keepdims=True)
        ix = jnp.min(jnp.where(logits == v, eidx, float(N_EXPERTS)), axis=0, keepdims=True)
        logits = jnp.where(eidx == ix, -jnp.inf, logits)
        vals.append(v)
        idxs.append(ix)
    exps = [jnp.exp(v - vals[0]) for v in vals]
    den = exps[0] + exps[1] + exps[2] + exps[3]
    route_ref[...] = jnp.concatenate(idxs + [ex / den for ex in exps], axis=0)


def post_attention(o, x2d, p2d, wo, bo, g1, b1, wrh, wrl, br, wpg, bpg, wpp):
    n = x2d.shape[0]
    tm = min(n, 512)
    row = lambda w: pl.BlockSpec((tm, w), lambda i: (i, 0))
    return pl.pallas_call(
        _post_attn_kernel,
        out_shape=(jax.ShapeDtypeStruct((n * ROW_SLABS, LANES), F32), jax.ShapeDtypeStruct((n, D_MODEL), F32),
                   jax.ShapeDtypeStruct((2 * TOP_K, n), F32)),
        grid=(n // tm,),
        in_specs=[row(D_MODEL), row(D_MODEL), row(P_DIM), _full((D_MODEL, D_MODEL)), _full((1, D_MODEL)),
                  _full((1, D_MODEL)), _full((1, D_MODEL)), _full((N_EXPERTS, D_MODEL)), _full((N_EXPERTS, D_MODEL)),
                  _full((N_EXPERTS, 1)), _full((D_MODEL, D_MODEL)), _full((1, D_MODEL)), _full((P_DIM, D_MODEL))],
        out_specs=(pl.BlockSpec((tm * ROW_SLABS, LANES), lambda i: (i, 0)), row(D_MODEL),
                   pl.BlockSpec((2 * TOP_K, tm), lambda i: (0, i))),
        compiler_params=_cparams(("parallel",)),
        name="post_attention",
    )(o, x2d, p2d, wo, bo, g1, b1, wrh, wrl, br, wpg, bpg, wpp)


MOE_TILE = 256
MOE_K_CHUNKS = 4
DMA_ISSUE_UNROLL = 8


def _route_tables(expert_idx, tm):
    k, n = expert_idx.shape
    a = k * n
    n_rows = a + (N_EXPERTS + 1) * tm
    flat_e = expert_idx.reshape(a)
    order = jnp.argsort(flat_e, stable=True).astype(jnp.int32)
    cnt = jnp.sum((flat_e[:, None] == jnp.arange(N_EXPERTS, dtype=jnp.int32)[None, :]).astype(jnp.int32), axis=0)
    start = jnp.cumsum(cnt) - cnt
    pcnt = ((cnt + tm - 1) // tm) * tm
    pend = jnp.cumsum(pcnt)
    pstart = pend - pcnt
    q = jnp.arange(n_rows, dtype=jnp.int32)
    e_of_q = jnp.sum((q[:, None] >= pend[None, :]).astype(jnp.int32), axis=1)
    e_c = jnp.minimum(e_of_q, N_EXPERTS - 1)
    local = q - pstart[e_c]
    valid = (e_of_q < N_EXPERTS) & (local < cnt[e_c])
    r = jnp.clip(start[e_c] + local, 0, a - 1)
    src_tok = jnp.where(valid, order[r] % n, 0).astype(jnp.int32)
    inv = jnp.argsort(order).astype(jnp.int32)
    dest = (pstart[flat_e] + inv - start[flat_e]).astype(jnp.int32)
    nt = n_rows // tm
    tile_e = e_c[::tm]
    tile_active = (jnp.arange(nt, dtype=jnp.int32) * tm < pend[-1]).astype(jnp.int32)
    prev = jnp.concatenate([jnp.full((1,), -1, jnp.int32), tile_e[:-1]])
    tile_first = ((tile_e != prev) & (tile_active > 0)).astype(jnp.int32)
    return tile_e, tile_active, tile_first, src_tok.reshape(nt, 1, tm), dest.reshape(k, n)


def _moe_kernel(te_ref, act_ref, first_ref, tok_ref, tok_next_ref,
                x_hbm, w1_ref, b1_ref, w2_ref, b2_ref, y_ref,
                xbuf, w1b, w2b, sem):
    i = pl.program_id(0)
    tm = xbuf.shape[1] // ROW_SLABS
    slot = i % 2

    def row_copy(tref, r, s):
        src = pl.multiple_of(tref[0, 0, r] * ROW_SLABS, ROW_SLABS)
        return pltpu.make_async_copy(x_hbm.at[pl.ds(src, ROW_SLABS)], xbuf.at[s, pl.ds(r * ROW_SLABS, ROW_SLABS)], sem.at[s])

    def wait_slot(s):
        pltpu.make_async_copy(x_hbm.at[pl.ds(0, tm * ROW_SLABS)], xbuf.at[s], sem.at[s]).wait()

    active = act_ref[i] > 0
    requested = jnp.where(i == 0, act_ref[0], act_ref[jnp.maximum(i - 1, 0)]) > 0

    @pl.when((i == 0) & active)
    def _():
        def body(r, carry):
            row_copy(tok_ref, r, 0).start()
            return carry
        lax.fori_loop(0, tm, body, 0, unroll=DMA_ISSUE_UNROLL)

    @pl.when(active)
    def _():
        wait_slot(slot)

        @pl.when(first_ref[i] > 0)
        def _():
            w1b[...] = w1_ref[0].astype(BF16)
            w2b[...] = w2_ref[0].astype(BF16)

        per = tm // MOE_K_CHUNKS
        kw = D_MODEL // MOE_K_CHUNKS
        h = jnp.zeros((tm, 2 * D_FF), F32) + b1_ref[0]
        for kc in range(MOE_K_CHUNKS):
            for r in range(kc * per, (kc + 1) * per):
                row_copy(tok_next_ref, r, 1 - slot).start(priority=1)
            xk = jnp.concatenate([xbuf[slot, pl.ds(s, tm, stride=ROW_SLABS), :]
                                  for s in range(kc * kw // LANES, (kc + 1) * kw // LANES)], axis=1).astype(BF16)
            h = h + jnp.dot(xk, w1b[kc * kw:(kc + 1) * kw, :], preferred_element_type=F32)
        gate = jnp.minimum(h[:, :D_FF], SWIGLU_LIMIT)
        up = jnp.clip(h[:, D_FF:], -SWIGLU_LIMIT, SWIGLU_LIMIT)
        act = (up + 1.0) * gate * jax.nn.sigmoid(SWIGLU_ALPHA * gate)
        _store_row_slabs(y_ref, jnp.dot(act.astype(BF16), w2b[...], preferred_element_type=F32) + b2_ref[0])

    @pl.when(jnp.logical_not(active))
    def _():
        y_ref[...] = jnp.zeros_like(y_ref)

    @pl.when(requested & jnp.logical_not(active))
    def _():
        wait_slot(slot)


def routed_experts(x1, tile_e, tile_active, tile_first, src_tok, layer, w_e1, b_e1, w_e2, b_e2):
    nt, _, tm = src_tok.shape
    tokspec = lambda f: pl.BlockSpec((1, 1, tm), f, memory_space=pltpu.SMEM)
    grid_spec = pltpu.PrefetchScalarGridSpec(
        num_scalar_prefetch=3,
        grid=(nt,),
        in_specs=[tokspec(lambda i, *_: (i, 0, 0)),
                  tokspec(lambda i, *_: (jnp.minimum(i + 1, nt - 1), 0, 0)),
                  pl.BlockSpec(memory_space=pl.ANY),
                  pl.BlockSpec((None, 1, D_MODEL, 2 * D_FF), lambda i, te, *_: (layer, te[i], 0, 0)),
                  pl.BlockSpec((1, 1, 2 * D_FF), lambda i, te, *_: (te[i], 0, 0)),
                  pl.BlockSpec((None, 1, D_FF, D_MODEL), lambda i, te, *_: (layer, te[i], 0, 0)),
                  pl.BlockSpec((1, 1, D_MODEL), lambda i, te, *_: (te[i], 0, 0))],
        out_specs=pl.BlockSpec((tm * ROW_SLABS, LANES), lambda i, *_: (i, 0)),
        scratch_shapes=[pltpu.VMEM((2, tm * ROW_SLABS, LANES), F32), pltpu.VMEM((D_MODEL, 2 * D_FF), BF16),
                        pltpu.VMEM((D_FF, D_MODEL), BF16), pltpu.SemaphoreType.DMA((2,))],
    )
    return pl.pallas_call(
        _moe_kernel,
        out_shape=jax.ShapeDtypeStruct((nt * tm * ROW_SLABS, LANES), F32),
        grid_spec=grid_spec,
        compiler_params=_cparams(("arbitrary",)),
        name="routed_experts",
    )(tile_e, tile_active, tile_first, src_tok, src_tok, x1, w_e1, b_e1.reshape(N_EXPERTS, 1, 2 * D_FF),
      w_e2, b_e2.reshape(N_EXPERTS, 1, D_MODEL))


COMBINE_TILE = 128


def _combine_kernel(pos_ref, pos_next_ref, x1_ref, e_ref, gw_ref, g2_ref, b2_ref, ys_hbm, out_ref, buf, sem):
    i = pl.program_id(0)
    nt = pl.num_programs(0)
    tm = out_ref.shape[0]
    n_rows = TOP_K * tm
    slot = i % 2

    def row_copy(pref, r, s):
        src = pl.multiple_of(pref[0, 0, r] * ROW_SLABS, ROW_SLABS)
        return pltpu.make_async_copy(ys_hbm.at[pl.ds(src, ROW_SLABS)], buf.at[s, pl.ds(r * ROW_SLABS, ROW_SLABS)], sem.at[s])

    def wait_slot(s):
        pltpu.make_async_copy(ys_hbm.at[pl.ds(0, n_rows * ROW_SLABS)], buf.at[s], sem.at[s]).wait()

    @pl.when(i == 0)
    def _():
        def body(r, carry):
            row_copy(pos_ref, r, 0).start()
            return carry
        lax.fori_loop(0, n_rows, body, 0, unroll=DMA_ISSUE_UNROLL)

    wait_slot(slot)
    for r in range(n_rows):
        row_copy(pos_next_ref, r, 1 - slot).start(priority=r % 2)

    gw = gw_ref[...]
    m = jnp.zeros((tm, D_MODEL), F32)
    for k in range(TOP_K):
        m = m + gw[:, k:k + 1] * _load_row_slabs(buf.at[slot], k * tm, tm)
    out_ref[...] = _layer_norm(DN_ALPHA * _load_row_slabs(x1_ref, 0, tm) + m + e_ref[...], g2_ref[...], b2_ref[...])

    @pl.when(i == nt - 1)
    def _():
        wait_slot(1 - slot)


def combine_experts(pos, x1, e, gw, g2, b2, ys):
    n = e.shape[0]
    tm = min(n, COMBINE_TILE)
    nt = n // tm
    pos_t = pos.reshape(TOP_K, nt, tm).transpose(1, 0, 2).reshape(nt, 1, TOP_K * tm)
    posspec = lambda f: pl.BlockSpec((1, 1, TOP_K * tm), f, memory_space=pltpu.SMEM)
    row = lambda w: pl.BlockSpec((tm, w), lambda i: (i, 0))
    return pl.pallas_call(
        _combine_kernel,
        out_shape=jax.ShapeDtypeStruct((n, D_MODEL), F32),
        grid=(nt,),
        in_specs=[posspec(lambda i: (i, 0, 0)), posspec(lambda i: (jnp.minimum(i + 1, nt - 1), 0, 0)),
                  pl.BlockSpec((tm * ROW_SLABS, LANES), lambda i: (i, 0)), row(D_MODEL), row(TOP_K),
                  _full((1, D_MODEL)), _full((1, D_MODEL)), pl.BlockSpec(memory_space=pl.ANY)],
        out_specs=row(D_MODEL),
        scratch_shapes=[pltpu.VMEM((2, TOP_K * tm * ROW_SLABS, LANES), F32), pltpu.SemaphoreType.DMA((2,))],
        compiler_params=_cparams(("arbitrary",)),
        name="combine_experts",
    )(pos_t, pos_t, x1, e, gw, g2, b2, ys)


def kernel(x, p, positions, w_in, b_in, ck_pe, w_ck1, w_ck2, cv_pe, w_cv1, w_cv2, sinks, w_o, b_o,
           ln1_g, ln1_b, w_r, b_r, w_e1, b_e1, w_e2, b_e2, w_pg, b_pg, w_pp, ln2_g, ln2_b):
    batch, seq, _ = x.shape
    n = batch * seq
    n_cmp = seq // CMP_STRIDE
    inv = 1.0 / (ROPE_THETA ** (jnp.arange(0, HEAD_DIM, 2, dtype=F32) / HEAD_DIM))
    inv_tiled = jnp.tile(inv, LANES // (HEAD_DIM // 2))[None, :]
    cos, sin = rope_tables(positions.reshape(n), inv_tiled)
    pos_c = jnp.concatenate([positions[:, CMP_BLOCK - 1::CMP_STRIDE], positions[:, -1:]], axis=1)
    cos_c, sin_c = rope_tables(pos_c.reshape(batch * n_cmp), inv_tiled)
    pool = jnp.asarray(_pool_matrix(), BF16)

    x2d = x.reshape(n, D_MODEL)
    for li in range(DEPTH):
        qn, qs, ks, kw, kswa, kc, vc, vst, vwt, vswat, gates_t = input_projection(
            x2d, *_prep_inproj_weights(w_in[li], b_in[li]), cos, sin)
        kweights = _prep_compress_weights(ck_pe[li], w_ck1[li], w_ck2[li], True)
        vweights = _prep_compress_weights(cv_pe[li], w_cv1[li], w_cv2[li], False)
        kc_c, vc_ct = compress_kv(kc.reshape(batch * n_cmp, CMP_STRIDE * KV_W), vc.reshape(batch * n_cmp, CMP_STRIDE * KV_W),
                                  cos_c, sin_c, kweights, vweights)
        o = attention(sinks[li], qn, qs, gates_t, kc_c, vc_ct, ks, vst, kw, vwt, kswa, vswat, pool, batch, seq)

        wr_t = w_r[li].T
        wrh = wr_t.astype(BF16)
        wrl = (wr_t - wrh.astype(F32)).astype(BF16)
        x1, e, route = post_attention(o, x2d, p[li].reshape(n, P_DIM), w_o[li].astype(BF16), b_o[li][None, :],
                                      ln1_g[li][None, :], ln1_b[li][None, :], wrh, wrl, b_r[li][:, None],
                                      w_pg[li].astype(BF16), b_pg[li][None, :], w_pp[li].astype(BF16))
        expert_idx = route[:TOP_K].astype(jnp.int32)
        gw = route[TOP_K:].T
        tile_e, tile_active, tile_first, src_tok, pos = _route_tables(expert_idx, MOE_TILE)
        ys = routed_experts(x1, tile_e, tile_active, tile_first, src_tok, li, w_e1, b_e1[li], w_e2, b_e2[li])
        x2d = combine_experts(pos, x1, e, gw, ln2_g[li][None, :], ln2_b[li][None, :], ys)
    return x2d.reshape(batch, seq, D_MODEL)
```
